```python
import math
import jax
import jax.numpy as jnp
from jax import lax
import numpy as np

D_MODEL = 2048
BATCH = 2
SEQ = 16384
DEPTH = 2

N_META = 16
EPS = 1e-6
ROPE_THETA = 500000.0
ROPE_FRACTION = 4

LRU_WIDTH = D_MODEL // 2
LRU_BLOCKS = 4
LRU_BLOCK = LRU_WIDTH // LRU_BLOCKS
CONV_W = 4
LRU_C = 8.0

GLA_HEADS = 4
GLA_DV = (D_MODEL // 2) // GLA_HEADS
GLA_DK = GLA_DV // 2
GLA_RANK = 16
GLA_GATE_NORM = 16.0
GLA_CHUNK = 64

ATT_HEADS = 16
ATT_KV_HEADS = 4
HEAD_DIM = D_MODEL // ATT_HEADS
ATT_GROUPS = ATT_HEADS // ATT_KV_HEADS
IDX_HEADS = 8
IDX_DIM = 64
TOPK_MAX = 256
Q_BLOCK = 128
BIG = 1e30
NEG = -1e30

D_FF = ((8 * D_MODEL // 3 + 255) // 256) * 256
N_EXPERTS = 8
TOP_K_EXPERTS = 2
D_EXPERT = 7 * D_MODEL // 2
MOE_BLOCK = 512

AB_SPLITS = (LRU_WIDTH, LRU_WIDTH, GLA_HEADS * GLA_DK, GLA_HEADS * GLA_DK,
             GLA_HEADS * GLA_DV, GLA_HEADS * GLA_DV, GLA_RANK)
D_IN_AB = sum(AB_SPLITS)
C_SPLITS = (ATT_HEADS * HEAD_DIM, ATT_KV_HEADS * HEAD_DIM, ATT_KV_HEADS * HEAD_DIM,
            IDX_HEADS * IDX_DIM, IDX_DIM, IDX_HEADS)
D_IN_C = sum(C_SPLITS)

kernel_name = "hybrid_rglru_gla_dsa_moe_meta"


def rms_norm(x, g):
    xf = x.astype(jnp.float32)
    y = xf * lax.rsqrt(jnp.mean(xf * xf, axis=-1, keepdims=True) + EPS)
    return (y * g.astype(jnp.float32)).astype(x.dtype)


def split_cols(z, sizes):
    offs = [int(o) for o in np.cumsum(sizes)[:-1]]
    return jnp.split(z, offs, axis=-1)


def partial_rope(x, pos):
    rot = x.shape[-1] // ROPE_FRACTION
    half = rot // 2
    freqs = ROPE_THETA ** (-jnp.arange(half, dtype=jnp.float32) / half)
    ang = pos.astype(jnp.float32)[:, None] * freqs[None, :]
    cos = jnp.cos(ang)[:, None, :]
    sin = jnp.sin(ang)[:, None, :]
    x1 = x[..., :half].astype(jnp.float32)
    x2 = x[..., half:rot].astype(jnp.float32)
    xr = jnp.concatenate([x1 * cos - x2 * sin, x2 * cos + x1 * sin], axis=-1).astype(x.dtype)
    return jnp.concatenate([xr, x[..., rot:]], axis=-1)


def _linear_combine(c1, c2):
    a1, b1 = c1
    a2, b2 = c2
    return a1 * a2, a2 * b1 + b2


def rg_lru(xb, yb, conv_w, conv_b, w_r, b_r, w_i, b_i, lam):
    B, T, W = xb.shape
    xc = lax.conv_general_dilated(
        xb, conv_w[:, None, :].astype(xb.dtype), (1,), ((CONV_W - 1, 0),),
        dimension_numbers=("NWC", "WIO", "NWC"), feature_group_count=W) + conv_b
    xblk = xc.reshape(B, T, LRU_BLOCKS, LRU_BLOCK)
    r = jax.nn.sigmoid(jnp.einsum("btnc,ncd->btnd", xblk, w_r) + b_r).reshape(B, T, W)
    i = jax.nn.sigmoid(jnp.einsum("btnc,ncd->btnd", xblk, w_i) + b_i).reshape(B, T, W)
    log_a = -LRU_C * r.astype(jnp.float32) * jax.nn.softplus(-lam.astype(jnp.float32))
    a = jnp.exp(log_a)
    u = jnp.sqrt(-jnp.expm1(2.0 * log_a)) * (i * xc).astype(jnp.float32)
    _, h = lax.associative_scan(_linear_combine, (a, u), axis=1)
    return h.astype(xb.dtype) * jax.nn.gelu(yb)


def gla(q, k, v, g, glr, w_gk, b_gk, norm_g):
    B, T, _ = q.shape
    H, C = GLA_HEADS, GLA_CHUNK
    gk = jax.nn.log_sigmoid((glr @ w_gk + b_gk).astype(jnp.float32)) / GLA_GATE_NORM
    pad = (-T) % C
    n_c = (T + pad) // C

    def chunks(a, d):
        a = jnp.pad(a.astype(jnp.float32), ((0, 0), (pad, 0), (0, 0)))
        return a.reshape(B, n_c, C, H, d).transpose(1, 0, 3, 2, 4)

    qc = chunks(q, GLA_DK) * (GLA_DK ** -0.5)
    kc = chunks(k, GLA_DK)
    vc = chunks(v, GLA_DV)
    gc = chunks(gk, GLA_DK)
    tri = jnp.tril(jnp.ones((C, C), dtype=bool))

    def step(S, inp):
        qq, kk, vv, gg = inp
        bcum = jnp.cumsum(gg, axis=-2)
        inter = jnp.einsum("bhtd,bhde->bhte", qq * jnp.exp(bcum), S)
        diff = bcum[:, :, :, None, :] - bcum[:, :, None, :, :]
        decay = jnp.where(tri[:, :, None], jnp.exp(jnp.minimum(diff, 0.0)), 0.0)
        att = jnp.einsum("bhtd,bhsd,bhtsd->bhts", qq, kk, decay)
        intra = jnp.einsum("bhts,bhse->bhte", att, vv)
        blast = bcum[:, :, -1:, :]
        S_new = (jnp.exp(blast[:, :, 0, :])[..., None] * S
                 + jnp.einsum("bhsd,bhse->bhde", kk * jnp.exp(blast - bcum), vv))
        return S_new, inter + intra

    S0 = jnp.zeros((B, H, GLA_DK, GLA_DV), jnp.float32)
    _, o = lax.scan(step, S0, (qc, kc, vc, gc))
    o = o.transpose(1, 0, 3, 2, 4).reshape(B, n_c * C, H, GLA_DV)[:, pad:]
    o = o * lax.rsqrt(jnp.mean(o * o, axis=-1, keepdims=True) + EPS) * norm_g.astype(jnp.float32)
    return o.reshape(B, T, H * GLA_DV).astype(q.dtype) * jax.nn.silu(g)


def mixer_ab(xn, w_in, conv_w, conv_b, w_r, b_r, w_i, b_i, lam, w_gk, b_gk, gla_norm, w_out):
    xb, yb, q, k, v, g, glr = split_cols(xn @ w_in, AB_SPLITS)
    a_out = rg_lru(xb, yb, conv_w, conv_b, w_r, b_r, w_i, b_i, lam)
    b_out = gla(q, k, v, g, glr, w_gk, b_gk, gla_norm)
    return jnp.concatenate([a_out, b_out], axis=-1) @ w_out


def dsa_attention(xn, w_in, w_out):
    B, T, _ = xn.shape
    q, k, v, qi, ki, wi = split_cols(xn @ w_in, C_SPLITS)
    pos = jnp.arange(T)
    q = partial_rope(q.reshape(B, T, ATT_HEADS, HEAD_DIM), pos)
    k = partial_rope(k.reshape(B, T, ATT_KV_HEADS, HEAD_DIM), pos)
    v = v.reshape(B, T, ATT_KV_HEADS, HEAD_DIM)
    qi = partial_rope(qi.reshape(B, T, IDX_HEADS, IDX_DIM), pos).astype(jnp.float32)
    ki = partial_rope(ki.reshape(B, T, 1, IDX_DIM), pos)[:, :, 0].astype(jnp.float32)
    wi = wi.astype(jnp.float32) * (IDX_HEADS ** -0.5 * IDX_DIM ** -0.5)
    k_sel = min(TOPK_MAX, (T - N_META) // 4)
    n_blk = -(-T // Q_BLOCK)
    t_pad = n_blk * Q_BLOCK
    pad = t_pad - T

    def pad_t(a):
        return jnp.pad(a, ((0, 0), (0, pad)) + ((0, 0),) * (a.ndim - 2))

    q, k, v, qi, ki, wi = pad_t(q), pad_t(k), pad_t(v), pad_t(qi), pad_t(ki), pad_t(wi)
    key_pos = jnp.arange(t_pad)
    is_meta = key_pos < N_META
    bidx = jnp.arange(B)[:, None, None]

    def block(bi):
        t0 = bi * Q_BLOCK
        qb = lax.dynamic_slice_in_dim(q, t0, Q_BLOCK, axis=1)
        qib = lax.dynamic_slice_in_dim(qi, t0, Q_BLOCK, axis=1)
        wib = lax.dynamic_slice_in_dim(wi, t0, Q_BLOCK, axis=1)
        qpos = t0 + jnp.arange(Q_BLOCK)
        logits = jnp.einsum("bqhd,bsd->bqhs", qib, ki)
        score = jnp.einsum("bqh,bqhs->bqs", wib, jax.nn.relu(logits))
        causal = key_pos[None, :] <= qpos[:, None]
        score = jnp.where(causal, jnp.where(is_meta, BIG, score), NEG)
        vals, idx = lax.top_k(score, k_sel)
        valid = vals > 0.5 * NEG
        kg = k[bidx, idx]
        vg = v[bidx, idx]
        qg = qb.reshape(B, Q_BLOCK, ATT_KV_HEADS, ATT_GROUPS, HEAD_DIM)
        s = jnp.einsum("bqhgd,bqkhd->bqhgk", qg, kg).astype(jnp.float32) * (HEAD_DIM ** -0.5)
        s = jnp.where(valid[:, :, None, None, :], s, NEG)
        p = jax.nn.softmax(s, axis=-1).astype(vg.dtype)
        o = jnp.einsum("bqhgk,bqkhd->bqhgd", p, vg)
        return o.reshape(B, Q_BLOCK, ATT_HEADS * HEAD_DIM)

    o = lax.map(block, jnp.arange(n_blk))
    o = o.transpose(1, 0, 2, 3).reshape(B, t_pad, ATT_HEADS * HEAD_DIM)[:, :T]
    return o @ w_out


def swiglu(xn, w_gate, w_up, w_down):
    return (jax.nn.silu(xn @ w_gate) * (xn @ w_up)) @ w_down


def moe_ffn(xn, w_router, w_gate, w_up, w_down):
    B, T, D = xn.shape
    N = B * T
    xf = xn.reshape(N, D)
    logits = (xf @ w_router).astype(jnp.float32)
    top_vals, top_idx = lax.top_k(logits, TOP_K_EXPERTS)
    gates = jax.nn.softmax(top_vals, axis=-1)
    n_assign = N * TOP_K_EXPERTS
    e_flat = top_idx.reshape(-1)
    tok_flat = jnp.repeat(jnp.arange(N, dtype=jnp.int32), TOP_K_EXPERTS)
    g_flat = gates.reshape(-1)
    order = jnp.argsort(e_flat, stable=True)
    e_sorted, tok_sorted, g_sorted = e_flat[order], tok_flat[order], g_flat[order]
    counts = jnp.bincount(e_flat, length=N_EXPERTS)
    padded = (counts + MOE_BLOCK - 1) // MOE_BLOCK * MOE_BLOCK
    grp_start = jnp.cumsum(counts) - counts
    pad_end = jnp.cumsum(padded)
    pad_start = pad_end - padded
    dest = pad_start[e_sorted] + (jnp.arange(n_assign) - grp_start[e_sorted])
    n_blocks = -(-n_assign // MOE_BLOCK) + N_EXPERTS
    n_slots = n_blocks * MOE_BLOCK
    slot_tok = jnp.full((n_slots,), N, jnp.int32).at[dest].set(tok_sorted)
    slot_gate = jnp.zeros((n_slots,), jnp.float32).at[dest].set(g_sorted)
    blk_expert = jnp.minimum(
        jnp.searchsorted(pad_end, jnp.arange(n_blocks) * MOE_BLOCK, side="right"), N_EXPERTS - 1)
    x_pad = jnp.concatenate([xf, jnp.zeros((1, D), xf.dtype)], axis=0)

    def expert_block(args):
        tok, e = args
        xb = x_pad[tok]
        return swiglu(xb, w_gate[e], w_up[e], w_down[e])

    y = lax.map(expert_block, (slot_tok.reshape(n_blocks, MOE_BLOCK), blk_expert))
    y = y.reshape(n_slots, D) * slot_gate[:, None].astype(y.dtype)
    out = jnp.zeros((N + 1, D), y.dtype).at[slot_tok].add(y)[:N]
    return out.reshape(B, T, D)


def setup_inputs(seed: int = 0) -> dict:
    key = jax.random.key(seed)
    ks = list(jax.random.split(key, 32))
    f32 = jnp.float32
    ne = (DEPTH + 1) // 2
    no = DEPTH // 2

    def nrm(shape, fan_in):
        return jax.random.normal(ks.pop(), shape, f32) * (fan_in ** -0.5)

    def gain(shape):
        return 1.0 + 0.02 * jax.random.normal(ks.pop(), shape, f32)

    def bias(shape, s=0.02):
        return s * jax.random.normal(ks.pop(), shape, f32)

    x = jax.random.normal(ks.pop(), (BATCH, SEQ, D_MODEL), f32)
    meta = jax.random.normal(ks.pop(), (N_META, D_MODEL), f32)
    a0 = jax.random.uniform(ks.pop(), (ne, LRU_WIDTH), f32, minval=0.9, maxval=0.999)
    lru_lam = jnp.log(a0) - jnp.log1p(-a0)
    return {
        "x": x,
        "meta": meta,
        "ab_norm": gain((ne, D_MODEL)),
        "ab_w_in": nrm((ne, D_MODEL, D_IN_AB), D_MODEL),
        "lru_conv_w": nrm((ne, CONV_W, LRU_WIDTH), CONV_W),
        "lru_conv_b": bias((ne, LRU_WIDTH)),
        "lru_w_r": nrm((ne, LRU_BLOCKS, LRU_BLOCK, LRU_BLOCK), LRU_BLOCK),
        "lru_b_r": bias((ne, LRU_BLOCKS, LRU_BLOCK)),
        "lru_w_i": nrm((ne, LRU_BLOCKS, LRU_BLOCK, LRU_BLOCK), LRU_BLOCK),
        "lru_b_i": bias((ne, LRU_BLOCKS, LRU_BLOCK)),
        "lru_lam": lru_lam,
        "gla_w_gk": nrm((ne, GLA_RANK, GLA_HEADS * GLA_DK), GLA_RANK),
        "gla_b_gk": bias((ne, GLA_HEADS * GLA_DK), 0.1),
        "gla_norm": gain((ne, GLA_DV)),
        "ab_w_out": nrm((ne, D_MODEL, D_MODEL), D_MODEL),
        "ffn_norm": gain((ne, D_MODEL)),
        "ffn_w_gate": nrm((ne, D_MODEL, D_FF), D_MODEL),
        "ffn_w_up": nrm((ne, D_MODEL, D_FF), D_MODEL),
        "ffn_w_down": nrm((ne, D_FF, D_MODEL), D_FF),
        "c_norm": gain((no, D_MODEL)),
        "c_w_in": nrm((no, D_MODEL, D_IN_C), D_MODEL),
        "c_w_out": nrm((no, D_MODEL, D_MODEL), D_MODEL),
        "moe_norm": gain((no, D_MODEL)),
        "moe_router": nrm((no, D_MODEL, N_EXPERTS), D_MODEL),
        "moe_w_gate": nrm((no, N_EXPERTS, D_MODEL, D_EXPERT), D_MODEL),
        "moe_w_up": nrm((no, N_EXPERTS, D_MODEL, D_EXPERT), D_MODEL),
        "moe_w_down": nrm((no, N_EXPERTS, D_EXPERT, D_MODEL), D_EXPERT),
        "final_norm": gain((D_MODEL,)),
    }


def reference(x, meta, ab_norm, ab_w_in, lru_conv_w, lru_conv_b, lru_w_r, lru_b_r,
              lru_w_i, lru_b_i, lru_lam, gla_w_gk, gla_b_gk, gla_norm, ab_w_out,
              ffn_norm, ffn_w_gate, ffn_w_up, ffn_w_down, c_norm, c_w_in, c_w_out,
              moe_norm, moe_router, moe_w_gate, moe_w_up, moe_w_down, final_norm):
    B = x.shape[0]
    h = jnp.concatenate(
        [jnp.broadcast_to(meta.astype(x.dtype)[None], (B, N_META, x.shape[-1])), x], axis=1)
    for layer in range(DEPTH):
        i = layer // 2
        if layer % 2 == 0:
            h = h + mixer_ab(rms_norm(h, ab_norm[i]), ab_w_in[i], lru_conv_w[i], lru_conv_b[i],
                             lru_w_r[i], lru_b_r[i], lru_w_i[i], lru_b_i[i], lru_lam[i],
                             gla_w_gk[i], gla_b_gk[i], gla_norm[i], ab_w_out[i])
            h = h + swiglu(rms_norm(h, ffn_norm[i]), ffn_w_gate[i], ffn_w_up[i], ffn_w_down[i])
        else:
            h = h + dsa_attention(rms_norm(h, c_norm[i]), c_w_in[i], c_w_out[i])
            h = h + moe_ffn(rms_norm(h, moe_norm[i]), moe_router[i], moe_w_gate[i],
                            moe_w_up[i], moe_w_down[i])
    return rms_norm(h, final_norm)[:, N_META:]
```

```python
import functools

import jax
import jax.numpy as jnp
from jax import lax
from jax.experimental import pallas as pl
from jax.experimental.pallas import tpu as pltpu

F32 = jnp.float32
BF16 = jnp.bfloat16
I32 = jnp.int32

N_META_TOK = 16
EPS = 1e-6
ROPE_THETA = 500000.0
LRU_BLOCKS = 4
CONV_W = 4
LRU_C = 8.0
GLA_HEADS = 4
GLA_RANK = 16
GLA_GATE_NORM = 16.0
ATT_HEADS = 16
ATT_KV_HEADS = 4
ATT_GROUPS = ATT_HEADS // ATT_KV_HEADS
IDX_HEADS = 8
IDX_DIM = 64
TOPK_MAX = 256
BIG = 1e30
NEG = -1e30
N_EXPERTS = 8

LANES = 128
SUBLANES = 8
TIME_TILE = 384
ROW_TILE_BIG = 768
COL_TILE = 512
GLA_CHUNK = 64
GATHER_ROWS = 512
VMEM_LIMIT = 56 * 1024 * 1024
INT_MIN = -(2 ** 31)


def _cparams(sem):
    return pltpu.CompilerParams(dimension_semantics=sem, vmem_limit_bytes=VMEM_LIMIT)


def _rms(x, g):
    return x * lax.rsqrt(jnp.mean(x * x, axis=-1, keepdims=True) + EPS) * g


def _sigmoid(x):
    return 1.0 / (1.0 + jnp.exp(-x))


def _pick_tile(n, candidates):
    for c in candidates:
        if n % c == 0:
            return c
    raise ValueError(f"no tile in {candidates} divides {n}")


def _norm_matmul_kernel(x_ref, g_ref, w_ref, o_ref, xn_ref):
    @pl.when(pl.program_id(1) == 0)
    def _():
        xn_ref[...] = _rms(x_ref[...], g_ref[...]).astype(BF16)

    o_ref[...] = jnp.dot(xn_ref[...], w_ref[...], preferred_element_type=F32).astype(o_ref.dtype)


def norm_matmul(x, g, w):
    m, d = x.shape
    n = w.shape[1]
    tm = _pick_tile(m, (ROW_TILE_BIG, TIME_TILE))
    tn = COL_TILE
    return pl.pallas_call(
        _norm_matmul_kernel,
        out_shape=jax.ShapeDtypeStruct((m, n), BF16),
        grid=(m // tm, n // tn),
        in_specs=[pl.BlockSpec((tm, d), lambda i, j: (i, 0)),
                  pl.BlockSpec((1, d), lambda i, j: (0, 0)),
                  pl.BlockSpec((d, tn), lambda i, j: (0, j))],
        out_specs=pl.BlockSpec((tm, tn), lambda i, j: (i, j)),
        scratch_shapes=[pltpu.VMEM((tm, d), BF16)],
        compiler_params=_cparams(("parallel", "arbitrary")),
        name="norm_matmul",
    )(x, g, w)


def _matmul_res_kernel(*refs, nparts):
    xs = refs[:nparts]
    w_ref, r_ref, o_ref = refs[nparts:]
    acc = r_ref[...]
    k0 = 0
    for x_ref in xs:
        kp = x_ref.shape[1]
        acc = acc + jnp.dot(x_ref[...], w_ref[k0:k0 + kp, :], preferred_element_type=F32)
        k0 += kp
    o_ref[...] = acc


def matmul_residual(parts, w, res):
    m, n = res.shape
    k = w.shape[0]
    tm = _pick_tile(m, (ROW_TILE_BIG, TIME_TILE))
    tn = 1024
    in_specs = [pl.BlockSpec((tm, p.shape[1]), lambda i, j: (i, 0)) for p in parts]
    in_specs += [pl.BlockSpec((k, tn), lambda i, j: (0, j)),
                 pl.BlockSpec((tm, tn), lambda i, j: (i, j))]
    return pl.pallas_call(
        functools.partial(_matmul_res_kernel, nparts=len(parts)),
        out_shape=jax.ShapeDtypeStruct((m, n), F32),
        grid=(m // tm, n // tn),
        in_specs=in_specs,
        out_specs=pl.BlockSpec((tm, tn), lambda i, j: (i, j)),
        compiler_params=_cparams(("parallel", "arbitrary")),
        name="matmul_residual",
    )(*parts, w, res)


def _ffn_kernel(eid_ref, nused_ref, x_ref, g_ref, wg_ref, wu_ref, wd_ref, o_ref, xn_ref, *, residual):
    i = pl.program_id(0)
    j = pl.program_id(1)

    @pl.when(i < nused_ref[0])
    def _():
        @pl.when(j == 0)
        def _():
            xn_ref[...] = _rms(x_ref[...], g_ref[...]).astype(BF16)

        xn = xn_ref[...]
        gate = jnp.dot(xn, wg_ref[0], preferred_element_type=F32)
        up = jnp.dot(xn, wu_ref[0], preferred_element_type=F32)
        act = (gate * _sigmoid(gate) * up).astype(BF16)
        part = jnp.dot(act, wd_ref[0], preferred_element_type=F32)

        @pl.when(j == 0)
        def _():
            o_ref[...] = part + x_ref[...] if residual else part

        @pl.when(j > 0)
        def _():
            o_ref[...] += part

    @pl.when((i >= nused_ref[0]) & (j == 0))
    def _():
        o_ref[...] = jnp.zeros_like(o_ref)


def ffn(x, g, wg, wu, wd, eid, nused, tm, residual):
    m, d = x.shape
    f = wg.shape[2]
    tf = COL_TILE
    nj = f // tf

    def jj(i, j, nused):
        return jnp.where(i < nused[0], j, nj - 1)

    grid_spec = pltpu.PrefetchScalarGridSpec(
        num_scalar_prefetch=2,
        grid=(m // tm, nj),
        in_specs=[pl.BlockSpec((tm, d), lambda i, j, eid, nu: (i, 0)),
                  pl.BlockSpec((1, d), lambda i, j, eid, nu: (0, 0)),
                  pl.BlockSpec((1, d, tf), lambda i, j, eid, nu: (eid[i], 0, jj(i, j, nu))),
                  pl.BlockSpec((1, d, tf), lambda i, j, eid, nu: (eid[i], 0, jj(i, j, nu))),
                  pl.BlockSpec((1, tf, d), lambda i, j, eid, nu: (eid[i], jj(i, j, nu), 0))],
        out_specs=pl.BlockSpec((tm, d), lambda i, j, eid, nu: (i, 0)),
        scratch_shapes=[pltpu.VMEM((tm, d), BF16)],
    )
    return pl.pallas_call(
        functools.partial(_ffn_kernel, residual=residual),
        out_shape=jax.ShapeDtypeStruct((m, d), F32),
        grid_spec=grid_spec,
        compiler_params=_cparams(("parallel", "arbitrary")),
        name="ffn_residual" if residual else "ffn_experts",
    )(eid, nused, x, g, wg, wu, wd)


def _lru_kernel(xb_ref, yb_ref, cw_ref, cb_ref, wr_ref, br_ref, wi_ref, bi_ref, lam_ref,
                o_ref, xext_ref, h_ref):
    tb, w = xb_ref.shape

    @pl.when(pl.program_id(1) == 0)
    def _():
        xext_ref[0:SUBLANES, :] = jnp.zeros((SUBLANES, w), F32)
        h_ref[...] = jnp.zeros_like(h_ref)

    xext_ref[SUBLANES:SUBLANES + tb, :] = xb_ref[...].astype(F32)
    cw = cw_ref[...]
    xc = cb_ref[...] + cw[CONV_W - 1:CONV_W, :] * xext_ref[SUBLANES:SUBLANES + tb, :]
    for j in range(CONV_W - 1):
        xc = xc + cw[j:j + 1, :] * xext_ref[pl.ds(SUBLANES - (CONV_W - 1) + j, tb), :]
    xext_ref[0:SUBLANES, :] = xext_ref[tb:tb + SUBLANES, :]

    xcb = xc.astype(BF16)
    nb = wr_ref.shape[0]
    bw = w // nb
    r_parts, i_parts = [], []
    for n in range(nb):
        xs = xcb[:, n * bw:(n + 1) * bw]
        r_parts.append(jnp.dot(xs, wr_ref[n], preferred_element_type=F32))
        i_parts.append(jnp.dot(xs, wi_ref[n], preferred_element_type=F32))
    r = _sigmoid(jnp.concatenate(r_parts, axis=1) + br_ref[...])
    ig = _sigmoid(jnp.concatenate(i_parts, axis=1) + bi_ref[...])

    neg_lam = -lam_ref[...]
    softplus = jnp.maximum(neg_lam, 0.0) + jnp.log1p(jnp.exp(-jnp.abs(neg_lam)))
    log_a = -LRU_C * r * softplus
    a = jnp.exp(log_a)
    u = jnp.sqrt(1.0 - a * a) * (ig * xc)

    row = lax.broadcasted_iota(I32, (tb, w), 0)
    s = 1
    while s < tb:
        keep = row >= s
        a_sh = pltpu.roll(a, s, 0)
        u_sh = pltpu.roll(u, s, 0)
        u = jnp.where(keep, a * u_sh + u, u)
        a = jnp.where(keep, a * a_sh, a)
        s *= 2
    h = a * h_ref[0:1, :] + u
    h_ref[0:1, :] = h[tb - 1:tb, :]

    y = yb_ref[...].astype(F32)
    gelu = 0.5 * y * (1.0 + jnp.tanh(0.7978845608028654 * (y + 0.044715 * (y * y * y))))
    o_ref[...] = (h * gelu).astype(o_ref.dtype)


def rg_lru(z, conv_w, conv_b, w_r, b_r, w_i, b_i, lam):
    b, tp, _ = z.shape
    w = conv_w.shape[1]
    tb = TIME_TILE
    nb = w_r.shape[0]
    bw = w // nb
    full = lambda shape: pl.BlockSpec(shape, lambda bi, ti: (0,) * len(shape))
    return pl.pallas_call(
        _lru_kernel,
        out_shape=jax.ShapeDtypeStruct((b, tp, w), BF16),
        grid=(b, tp // tb),
        in_specs=[pl.BlockSpec((None, tb, w), lambda bi, ti: (bi, ti, 0)),
                  pl.BlockSpec((None, tb, w), lambda bi, ti: (bi, ti, 1)),
                  full((CONV_W, w)), full((1, w)),
                  full((nb, bw, bw)), full((1, w)),
                  full((nb, bw, bw)), full((1, w)),
                  full((1, w))],
        out_specs=pl.BlockSpec((None, tb, w), lambda bi, ti: (bi, ti, 0)),
        scratch_shapes=[pltpu.VMEM((tb + 2 * SUBLANES, w), F32), pltpu.VMEM((SUBLANES, w), F32)],
        compiler_params=_cparams(("parallel", "arbitrary")),
        name="rg_lru",
    )(z, z, conv_w, conv_b, w_r, b_r, w_i, b_i, lam)


def _gla_kernel(q_ref, k_ref, v_ref, g_ref, glr_ref, wgk_ref, bgk_ref, ng_ref, o_ref, st_ref,
                *, chunk, heads):
    tb, hdk = q_ref.shape
    dk = hdk // heads
    dv = v_ref.shape[1] // heads
    c = chunk

    @pl.when(pl.program_id(1) == 0)
    def _():
        st_ref[...] = jnp.zeros_like(st_ref)

    row = lax.broadcasted_iota(I32, (c, hdk), 0)
    tri = lax.broadcasted_iota(I32, (c, c), 0) >= lax.broadcasted_iota(I32, (c, c), 1)
    nt = (((1,), (1,)), ((), ()))
    tn = (((0,), (0,)), ((), ()))

    def body(ci, carry):
        sl = pl.ds(pl.multiple_of(ci * c, c), c)
        pre = jnp.dot(glr_ref[sl, :], wgk_ref[...], preferred_element_type=F32) + bgk_ref[...]
        gk = (jnp.minimum(pre, 0.0) - jnp.log1p(jnp.exp(-jnp.abs(pre)))) * (1.0 / GLA_GATE_NORM)
        bc = gk
        s = 1
        while s < c:
            bc = bc + jnp.where(row >= s, pltpu.roll(bc, s, 0), 0.0)
            s *= 2
        q = q_ref[sl, :].astype(F32) * (dk ** -0.5)
        k = k_ref[sl, :].astype(F32)
        v = v_ref[sl, :]
        g = g_ref[sl, :].astype(F32)
        bmid = bc[c // 2:c // 2 + 1, :]
        blast = bc[c - 1:c, :]
        q_inter = (q * jnp.exp(bc)).astype(BF16)
        q_intra = (q * jnp.exp(bc - bmid)).astype(BF16)
        k_intra = (k * jnp.exp(bmid - bc)).astype(BF16)
        k_state = (k * jnp.exp(blast - bc)).astype(BF16)
        decay = jnp.exp(blast)
        outs = []
        for h in range(heads):
            ks = slice(h * dk, (h + 1) * dk)
            vs = slice(h * dv, (h + 1) * dv)
            st = st_ref[h]
            vh = v[:, vs]
            inter = lax.dot_general(q_inter[:, ks], st.astype(BF16), nt, preferred_element_type=F32)
            att = lax.dot_general(q_intra[:, ks], k_intra[:, ks], nt, preferred_element_type=F32)
            att = jnp.where(tri, att, 0.0).astype(BF16)
            o = inter + jnp.dot(att, vh, preferred_element_type=F32)
            st_ref[h] = st * decay[:, ks] + lax.dot_general(vh, k_state[:, ks], tn,
                                                           preferred_element_type=F32)
            o = _rms(o, ng_ref[...])
            gh = g[:, vs]
            outs.append(o * (gh * _sigmoid(gh)))
        o_ref[sl, :] = jnp.concatenate(outs, axis=1).astype(o_ref.dtype)
        return carry

    lax.fori_loop(0, tb // c, body, 0)


def gla(z, col0, w_gk, b_gk, norm_g, hdk, hdv):
    b, tp, _ = z.shape
    tb = TIME_TILE
    cq, ck = col0 // hdk, col0 // hdk + 1
    cv = (col0 + 2 * hdk) // hdv
    cg = cv + 1
    cl = (col0 + 2 * hdk + 2 * hdv) // LANES
    full = lambda shape: pl.BlockSpec(shape, lambda bi, ti: (0,) * len(shape))
    return pl.pallas_call(
        functools.partial(_gla_kernel, chunk=GLA_CHUNK, heads=GLA_HEADS),
        out_shape=jax.ShapeDtypeStruct((b, tp, hdv), BF16),
        grid=(b, tp // tb),
        in_specs=[pl.BlockSpec((None, tb, hdk), lambda bi, ti: (bi, ti, cq)),
                  pl.BlockSpec((None, tb, hdk), lambda bi, ti: (bi, ti, ck)),
                  pl.BlockSpec((None, tb, hdv), lambda bi, ti: (bi, ti, cv)),
                  pl.BlockSpec((None, tb, hdv), lambda bi, ti: (bi, ti, cg)),
                  pl.BlockSpec((None, tb, LANES), lambda bi, ti: (bi, ti, cl)),
                  full((LANES, hdk)), full((1, hdk)), full((1, hdv // GLA_HEADS))],
        out_specs=pl.BlockSpec((None, tb, hdv), lambda bi, ti: (bi, ti, 0)),
        scratch_shapes=[pltpu.VMEM((GLA_HEADS, hdv // GLA_HEADS, hdk // GLA_HEADS), F32)],
        compiler_params=_cparams(("parallel", "arbitrary")),
        name="gla",
    )(z, z, z, z, z, w_gk, b_gk, norm_g)


def _rope_tile(x, cos, sin_up, sin_dn, half):
    return x * cos + pltpu.roll(x, half, 1) * sin_up + pltpu.roll(x, LANES - half, 1) * sin_dn


def _rope_kernel(qk_ref, qi_ref, kw_ref, tab_ref, qk_out, qi_out, ki_out, wi_out):
    def tabs(kind):
        return tab_ref[3 * kind], tab_ref[3 * kind + 1], tab_ref[3 * kind + 2]

    ca, sa_up, sa_dn = tabs(0)
    for t in range(qk_ref.shape[1] // LANES):
        sl = slice(t * LANES, (t + 1) * LANES)
        qk_out[:, sl] = _rope_tile(qk_ref[:, sl].astype(F32), ca, sa_up, sa_dn, 16).astype(BF16)
    cb, sb_up, sb_dn = tabs(1)
    for t in range(qi_ref.shape[1] // LANES):
        sl = slice(t * LANES, (t + 1) * LANES)
        qi_out[:, sl] = _rope_tile(qi_ref[:, sl].astype(F32), cb, sb_up, sb_dn, 8).astype(BF16)
    ck, sk_up, sk_dn = tabs(2)
    kw = _rope_tile(kw_ref[...].astype(F32), ck, sk_up, sk_dn, 8)
    ki_out[...] = kw.astype(BF16)
    wi_out[...] = pltpu.roll(kw, LANES - IDX_DIM, 1)


def rope_tables(tp, head_dim):
    pos = jnp.arange(tp, dtype=F32)[:, None]
    lane = jnp.arange(LANES)

    def tables(dim, period, active):
        half = (dim // 4) // 2
        freqs = ROPE_THETA ** (-jnp.arange(half, dtype=F32) / half)
        ang = pos * freqs[None, :]
        cos, sin = jnp.cos(ang), jnp.sin(ang)
        within = lane % period
        fidx = within % half
        cos_l, sin_l = cos[:, fidx], sin[:, fidx]
        lo = (within < half) & active
        hi = (within >= half) & (within < 2 * half) & active
        c = jnp.where((lo | hi)[None, :], cos_l, 1.0)
        s_up = jnp.where(hi[None, :], sin_l, 0.0)
        s_dn = jnp.where(lo[None, :], -sin_l, 0.0)
        return c, s_up, s_dn

    all_on = jnp.ones((LANES,), bool)
    a = tables(head_dim, head_dim, all_on)
    bq = tables(IDX_DIM, IDX_DIM, all_on)
    ck, sk_up, sk_dn = tables(IDX_DIM, IDX_DIM, lane < IDX_DIM)
    wscale = IDX_HEADS ** -0.5 * IDX_DIM ** -0.5
    ck = jnp.where((lane >= IDX_DIM)[None, :],
                   jnp.where(lane < IDX_DIM + IDX_HEADS, wscale, 0.0)[None, :], ck)
    return jnp.stack([*a, *bq, ck, sk_up, sk_dn])


def rope(z, tabs, tp, nqk, nqi, col_qi, col_kw):
    m = z.shape[0]
    tm = TIME_TILE
    nper = tp // tm
    return pl.pallas_call(
        _rope_kernel,
        out_shape=[jax.ShapeDtypeStruct((m, nqk), BF16), jax.ShapeDtypeStruct((m, nqi), BF16),
                   jax.ShapeDtypeStruct((m, LANES), BF16), jax.ShapeDtypeStruct((m, LANES), F32)],
        grid=(m // tm,),
        in_specs=[pl.BlockSpec((tm, nqk), lambda i: (i, 0)),
                  pl.BlockSpec((tm, nqi), lambda i: (i, col_qi // nqi)),
                  pl.BlockSpec((tm, LANES), lambda i: (i, col_kw // LANES)),
                  pl.BlockSpec((9, tm, LANES), lambda i: (0, i % nper, 0))],
        out_specs=[pl.BlockSpec((tm, nqk), lambda i: (i, 0)),
                   pl.BlockSpec((tm, nqi), lambda i: (i, 0)),
                   pl.BlockSpec((tm, LANES), lambda i: (i, 0)),
                   pl.BlockSpec((tm, LANES), lambda i: (i, 0))],
        compiler_params=_cparams(("parallel",)),
        name="rope",
    )(z, z, z, tabs)


def _index_keys(qi_heads, w_cols, ki_blk, q0, k0):
    nt = (((1,), (1,)), ((), ()))
    score = None
    for qh, wh in zip(qi_heads, w_cols):
        lg = lax.dot_general(qh, ki_blk, nt, preferred_element_type=F32)
        term = wh * jnp.maximum(lg, 0.0)
        score = term if score is None else score + term
    tq, tk = score.shape
    qpos = q0 + lax.broadcasted_iota(I32, (tq, tk), 0)
    kpos = k0 + lax.broadcasted_iota(I32, (tq, tk), 1)
    score = jnp.where(kpos <= qpos, jnp.where(kpos < N_META_TOK, BIG, score), NEG)
    bits = pltpu.bitcast(score, I32)
    keys = jnp.where(bits < 0, bits ^ jnp.int32(0x7FFFFFFF), bits)
    return keys, kpos, score


def _split_index_heads(qi, wi):
    qi_heads = [qi[:, h * IDX_DIM:(h + 1) * IDX_DIM] for h in range(IDX_HEADS)]
    w_cols = [wi[:, h:h + 1] for h in range(IDX_HEADS)]
    return qi_heads, w_cols


def _indexer_kernel(qi_ref, wi_ref, ki_ref, o_ref, keys_ref, *, k_sel):
    tq = qi_ref.shape[0]
    tk = tq
    qb = pl.program_id(1)
    q0 = qb * tq
    nkv = qb + 1
    qi_heads, w_cols = _split_index_heads(qi_ref[...], wi_ref[...])

    def fill(kb, carry):
        k0 = pl.multiple_of(kb * tk, tk)
        keys, _, _ = _index_keys(qi_heads, w_cols, ki_ref[pl.ds(k0, tk), 0:IDX_DIM], q0, k0)
        keys_ref[:, pl.ds(k0, tk)] = keys
        return carry

    lax.fori_loop(0, nkv, fill, 0)

    def count(pred):
        def blk(kb, acc):
            k0 = pl.multiple_of(kb * tk, tk)
            keys = keys_ref[:, pl.ds(k0, tk)]
            kpos = k0 + lax.broadcasted_iota(I32, (tq, tk), 1)
            hit = jnp.where(pred(keys, kpos), 1, 0)
            for t in range(tk // LANES):
                acc = acc + hit[:, t * LANES:(t + 1) * LANES]
            return acc
        acc = lax.fori_loop(0, nkv, blk, jnp.zeros((tq, LANES), I32))
        return jnp.sum(acc, axis=1, keepdims=True)

    def bit_step(bi, thr):
        cand = thr + jnp.left_shift(jnp.int32(1), 31 - bi)
        n = count(lambda keys, kpos: keys >= cand)
        return jnp.where(n >= k_sel, cand, thr)

    thr = lax.fori_loop(0, 32, bit_step, jnp.full((tq, 1), INT_MIN, I32))
    n_ge = count(lambda keys, kpos: keys >= thr)
    lane = lax.broadcasted_iota(I32, (tq, LANES), 1)
    no_tie_cut = jnp.full((tq, 1), 1 << 30, I32)
    o_ref[...] = jnp.where(lane == 0, thr, no_tie_cut)

    @pl.when(jnp.max(n_ge) > k_sel)
    def _():
        n_gt = count(lambda keys, kpos: keys > thr)
        need = k_sel - n_gt

        def cut_step(bi, cut):
            cand = cut + jnp.left_shift(jnp.int32(1), 15 - bi)
            n = count(lambda keys, kpos: (keys == thr) & (kpos < cand))
            return jnp.where(n <= need, cand, cut)

        cut = lax.fori_loop(0, 16, cut_step, jnp.zeros((tq, 1), I32))
        cut = jnp.where(n_ge > k_sel, cut, no_tie_cut)
        o_ref[...] = jnp.where(lane == 0, thr, cut)


def indexer(qi, wi, ki, b, tp, k_sel):
    tq = TIME_TILE
    nq = tp // tq
    return pl.pallas_call(
        functools.partial(_indexer_kernel, k_sel=k_sel),
        out_shape=jax.ShapeDtypeStruct((b * tp, LANES), I32),
        grid=(b, nq),
        in_specs=[pl.BlockSpec((tq, qi.shape[1]), lambda bi, qb: (bi * nq + qb, 0)),
                  pl.BlockSpec((tq, LANES), lambda bi, qb: (bi * nq + qb, 0)),
                  pl.BlockSpec((tp, LANES), lambda bi, qb: (bi, 0))],
        out_specs=pl.BlockSpec((tq, LANES), lambda bi, qb: (bi * nq + qb, 0)),
        scratch_shapes=[pltpu.VMEM((tq, tp), I32)],
        compiler_params=_cparams(("parallel", "arbitrary")),
        name="indexer",
    )(qi, wi, ki)


def _attn_kernel(q_ref, k_ref, v_ref, qi_ref, wi_ref, ki_ref, sel_ref, o_ref,
                 m_ref, l_ref, acc_ref, *, scale):
    tq = q_ref.shape[0]
    tk = k_ref.shape[0]
    hd = k_ref.shape[1] // ATT_KV_HEADS
    qb = pl.program_id(1)
    kb = pl.program_id(2)
    nt = (((1,), (1,)), ((), ()))

    @pl.when(kb == 0)
    def _():
        m_ref[...] = jnp.full_like(m_ref, NEG)
        l_ref[...] = jnp.zeros_like(l_ref)
        acc_ref[...] = jnp.zeros_like(acc_ref)

    @pl.when(kb <= qb)
    def _():
        qi_heads, w_cols = _split_index_heads(qi_ref[...], wi_ref[...])
        keys, kpos, score = _index_keys(qi_heads, w_cols, ki_ref[:, 0:IDX_DIM], qb * tq, kb * tk)
        thr = sel_ref[:, 0:1]
        cut = sel_ref[:, 1:2]
        chosen = (keys > thr) | ((keys == thr) & (kpos < cut))
        bias = jnp.where(chosen & (score > 0.5 * NEG), 0.0, NEG)
        for kvh in range(ATT_KV_HEADS):
            kh = k_ref[:, kvh * hd:(kvh + 1) * hd]
            vh = v_ref[:, kvh * hd:(kvh + 1) * hd]
            for gi in range(ATT_GROUPS):
                hi = kvh * ATT_GROUPS + gi
                hs = slice(hi * hd, (hi + 1) * hd)
                s = lax.dot_general(q_ref[:, hs], kh, nt, preferred_element_type=F32) * scale + bias
                m_old = m_ref[hi]
                m_new = jnp.maximum(m_old, jnp.max(s, axis=1, keepdims=True))
                p = jnp.exp(s - m_new)
                alpha = jnp.exp(m_old - m_new)
                l_ref[hi] = alpha * l_ref[hi] + jnp.sum(p, axis=1, keepdims=True)
                acc_ref[:, hs] = alpha * acc_ref[:, hs] + jnp.dot(p.astype(BF16), vh,
                                                                  preferred_element_type=F32)
                m_ref[hi] = m_new

    @pl.when(kb == pl.num_programs(2) - 1)
    def _():
        for hi in range(ATT_HEADS):
            hs = slice(hi * hd, (hi + 1) * hd)
            o_ref[:, hs] = (acc_ref[:, hs] / l_ref[hi]).astype(o_ref.dtype)


def sparse_attention(qk, z, qi, wi, ki, sel, b, tp, d, col_v):
    tq = tk = TIME_TILE
    nq = tp // tq
    hd = d // ATT_HEADS
    kvw = ATT_KV_HEADS * hd
    kvrow = lambda bi, qb, kb: bi * nq + jnp.minimum(kb, qb)
    qrow = lambda bi, qb, kb: bi * nq + qb
    return pl.pallas_call(
        functools.partial(_attn_kernel, scale=hd ** -0.5),
        out_shape=jax.ShapeDtypeStruct((b * tp, d), BF16),
        grid=(b, nq, nq),
        in_specs=[pl.BlockSpec((tq, d), lambda bi, qb, kb: (qrow(bi, qb, kb), 0)),
                  pl.BlockSpec((tk, kvw), lambda bi, qb, kb: (kvrow(bi, qb, kb), d // kvw)),
                  pl.BlockSpec((tk, kvw), lambda bi, qb, kb: (kvrow(bi, qb, kb), col_v // kvw)),
                  pl.BlockSpec((tq, qi.shape[1]), lambda bi, qb, kb: (qrow(bi, qb, kb), 0)),
                  pl.BlockSpec((tq, LANES), lambda bi, qb, kb: (qrow(bi, qb, kb), 0)),
                  pl.BlockSpec((tk, LANES), lambda bi, qb, kb: (kvrow(bi, qb, kb), 0)),
                  pl.BlockSpec((tq, LANES), lambda bi, qb, kb: (qrow(bi, qb, kb), 0))],
        out_specs=pl.BlockSpec((tq, d), lambda bi, qb, kb: (qrow(bi, qb, kb), 0)),
        scratch_shapes=[pltpu.VMEM((ATT_HEADS, tq, 1), F32), pltpu.VMEM((ATT_HEADS, tq, 1), F32),
                        pltpu.VMEM((tq, d), F32)],
        compiler_params=_cparams(("parallel", "parallel", "arbitrary")),
        name="sparse_attention",
    )(qk, qk, z, qi, wi, ki, sel)


def _router_kernel(x_ref, g_ref, w_ref, o_ref):
    xn = _rms(x_ref[...], g_ref[...]).astype(BF16)
    logits = jnp.dot(xn, w_ref[...], preferred_element_type=F32)
    lane = lax.broadcasted_iota(I32, logits.shape, 1)
    logits = jnp.where(lane < N_EXPERTS, logits, -jnp.inf)
    v1 = jnp.max(logits, axis=1, keepdims=True)
    e1 = jnp.min(jnp.where(logits == v1, lane, LANES), axis=1, keepdims=True)
    rest = jnp.where(lane == e1, -jnp.inf, logits)
    v2 = jnp.max(rest, axis=1, keepdims=True)
    e2 = jnp.min(jnp.where(rest == v2, lane, LANES), axis=1, keepdims=True)
    ex = jnp.exp(v2 - v1)
    g1 = 1.0 / (1.0 + ex)
    g2 = ex / (1.0 + ex)
    out = jnp.where(lane == 0, g1, jnp.where(lane == 1, g2, 0.0))
    out = jnp.where(lane == 2, e1.astype(F32), jnp.where(lane == 3, e2.astype(F32), out))
    o_ref[...] = out


def router(x, g, w):
    m, d = x.shape
    tm = TIME_TILE
    return pl.pallas_call(
        _router_kernel,
        out_shape=jax.ShapeDtypeStruct((m, LANES), F32),
        grid=(m // tm,),
        in_specs=[pl.BlockSpec((tm, d), lambda i: (i, 0)),
                  pl.BlockSpec((1, d), lambda i: (0, 0)),
                  pl.BlockSpec((d, LANES), lambda i: (0, 0))],
        out_specs=pl.BlockSpec((tm, LANES), lambda i: (i, 0)),
        compiler_params=_cparams(("parallel",)),
        name="router",
    )(x, g, w)


def _gather_kernel(idx_ref, src_ref, dst_ref, sem, *, rows):
    base = pl.program_id(0) * rows

    def row_copy(r):
        return pltpu.make_async_copy(src_ref.at[pl.ds(idx_ref[base + r], 1)],
                                     dst_ref.at[pl.ds(base + r, 1)], sem)

    def start(r, carry):
        row_copy(r).start()
        return carry

    def wait(r, carry):
        row_copy(r).wait()
        return carry

    lax.fori_loop(0, rows, start, 0)
    lax.fori_loop(0, rows, wait, 0)


def gather_rows(src, idx):
    n = idx.shape[0]
    rows = _pick_tile(n, (GATHER_ROWS, TIME_TILE, 256, 128))
    grid_spec = pltpu.PrefetchScalarGridSpec(
        num_scalar_prefetch=1,
        grid=(n // rows,),
        in_specs=[pl.BlockSpec(memory_space=pl.ANY)],
        out_specs=pl.BlockSpec(memory_space=pl.ANY),
        scratch_shapes=[pltpu.SemaphoreType.DMA(())],
    )
    return pl.pallas_call(
        functools.partial(_gather_kernel, rows=rows),
        out_shape=jax.ShapeDtypeStruct((n, src.shape[1]), src.dtype),
        grid_spec=grid_spec,
        compiler_params=_cparams(("arbitrary",)),
        name="gather_rows",
    )(idx, src)


def _combine_norm_kernel(h_ref, y1_ref, y2_ref, route_ref, g_ref, o_ref):
    g1 = route_ref[:, 0:1]
    g2 = route_ref[:, 1:2]
    h = h_ref[...] + g1 * y1_ref[...] + g2 * y2_ref[...]
    o_ref[...] = _rms(h, g_ref[...])


def combine_norm(h, y, route, g):
    m, d = h.shape
    tm = TIME_TILE
    nblk = m // tm
    return pl.pallas_call(
        _combine_norm_kernel,
        out_shape=jax.ShapeDtypeStruct((m, d), F32),
        grid=(nblk,),
        in_specs=[pl.BlockSpec((tm, d), lambda i: (i, 0)),
                  pl.BlockSpec((tm, d), lambda i: (i, 0)),
                  pl.BlockSpec((tm, d), lambda i: (i + nblk, 0)),
                  pl.BlockSpec((tm, LANES), lambda i: (i, 0)),
                  pl.BlockSpec((1, d), lambda i: (0, 0))],
        out_specs=pl.BlockSpec((tm, d), lambda i: (i, 0)),
        compiler_params=_cparams(("parallel",)),
        name="combine_norm",
    )(h, y, y, route, g)


def _pad_cols(w, n):
    return jnp.pad(w, ((0, 0), (0, n - w.shape[1])))


def _round_up(n, k):
    return -(-n // k) * k


def layer_ab(h, b, tp, norm_g, w_in, conv_w, conv_b, w_r, b_r, w_i, b_i, lam, w_gk, b_gk,
             gla_norm, w_out, ffn_norm, ffn_wg, ffn_wu, ffn_wd):
    m, d = h.shape
    lw = conv_w.shape[1]
    hdk = w_gk.shape[1]
    hdv = d - lw
    n_in = _round_up(w_in.shape[1] - GLA_RANK + LANES, COL_TILE)
    z = norm_matmul(h, norm_g[None], _pad_cols(w_in, n_in).astype(BF16)).reshape(b, tp, n_in)
    a_out = rg_lru(z, conv_w, conv_b[None], w_r.astype(BF16), b_r.reshape(1, lw),
                   w_i.astype(BF16), b_i.reshape(1, lw), lam[None])
    w_gk_p = jnp.pad(w_gk, ((0, LANES - GLA_RANK), (0, 0))).astype(BF16)
    b_out = gla(z, 2 * lw, w_gk_p, b_gk[None], gla_norm[None], hdk, hdv)
    h = matmul_residual([a_out.reshape(m, lw), b_out.reshape(m, hdv)], w_out.astype(BF16), h)
    tm = _pick_tile(m, (ROW_TILE_BIG, TIME_TILE))
    nblk = m // tm
    return ffn(h, ffn_norm[None], ffn_wg.astype(BF16)[None], ffn_wu.astype(BF16)[None],
               ffn_wd.astype(BF16)[None], jnp.zeros((nblk,), I32), jnp.full((1,), nblk, I32),
               tm, residual=True)


def layer_c(h, b, tp, t_real, norm_g, w_in, w_out):
    m, d = h.shape
    hd = d // ATT_HEADS
    kvw = ATT_KV_HEADS * hd
    nqi = IDX_HEADS * IDX_DIM
    col_v = d + kvw
    col_qi = d + 2 * kvw
    col_kw = col_qi + nqi
    n_in = _round_up(col_kw + LANES, COL_TILE)
    z = norm_matmul(h, norm_g[None], _pad_cols(w_in, n_in).astype(BF16))
    qk, qi, ki, wi = rope(z, rope_tables(tp, hd), tp, d + kvw, nqi, col_qi, col_kw)
    k_sel = min(TOPK_MAX, (t_real - N_META_TOK) // 4)
    sel = indexer(qi, wi, ki, b, tp, k_sel)
    att = sparse_attention(qk, z, qi, wi, ki, sel, b, tp, d, col_v)
    return matmul_residual([att], w_out.astype(BF16), h)


def moe_dispatch(route, tm):
    m = route.shape[0]
    e_flat = route[:, 2:4].astype(I32).reshape(-1)
    onehot = (e_flat[:, None] == jnp.arange(N_EXPERTS)[None, :]).astype(I32)
    rank = jnp.take_along_axis(jnp.cumsum(onehot, axis=0) - onehot, e_flat[:, None], axis=1)[:, 0]
    counts = jnp.sum(onehot, axis=0)
    padded = (counts + tm - 1) // tm * tm
    pad_end = jnp.cumsum(padded)
    dest = (pad_end - padded)[e_flat] + rank
    nblk = -(-(2 * m) // tm) + N_EXPERTS
    slot_tok = jnp.zeros((nblk * tm,), I32).at[dest].set(jnp.arange(2 * m, dtype=I32) // 2)
    nused = (pad_end[-1] // tm).astype(I32)
    blk = jnp.minimum(jnp.arange(nblk, dtype=I32), nused - 1)
    blk_expert = jnp.minimum(jnp.searchsorted(pad_end, blk * tm, side="right"),
                             N_EXPERTS - 1).astype(I32)
    return slot_tok, dest.reshape(m, 2), blk_expert, nused.reshape(1)


def layer_moe_final(h, norm_g, w_router, wg, wu, wd, final_g):
    m, d = h.shape
    route = router(h, norm_g[None], _pad_cols(w_router, LANES).astype(BF16))
    tm = _pick_tile(m, (ROW_TILE_BIG, TIME_TILE))
    slot_tok, tok_slots, blk_expert, nused = moe_dispatch(route, tm)
    x_slots = gather_rows(h, slot_tok)
    y_slots = ffn(x_slots, norm_g[None], wg.astype(BF16), wu.astype(BF16), wd.astype(BF16),
                  blk_expert, nused, tm, residual=False)
    y_tok = gather_rows(y_slots, tok_slots.T.reshape(-1))
    return combine_norm(h, y_tok, route, final_g[None])


def kernel(x, meta, ab_norm, ab_w_in, lru_conv_w, lru_conv_b, lru_w_r, lru_b_r, lru_w_i, lru_b_i,
           lru_lam, gla_w_gk, gla_b_gk, gla_norm, ab_w_out, ffn_norm, ffn_w_gate, ffn_w_up,
           ffn_w_down, c_norm, c_w_in, c_w_out, moe_norm, moe_router, moe_w_gate, moe_w_up,
           moe_w_down, final_norm):
    b, seq, d = x.shape
    assert ab_norm.shape[0] == 1 and c_norm.shape[0] == 1, "two-layer trunk only"
    t_real = seq + N_META_TOK
    tp = _round_up(t_real, TIME_TILE)
    h = jnp.concatenate([jnp.broadcast_to(meta.astype(x.dtype)[None], (b, N_META_TOK, d)), x,
                         jnp.zeros((b, tp - t_real, d), x.dtype)], axis=1).reshape(b * tp, d)
    h = layer_ab(h, b, tp, ab_norm[0], ab_w_in[0], lru_conv_w[0], lru_conv_b[0], lru_w_r[0],
                 lru_b_r[0], lru_w_i[0], lru_b_i[0], lru_lam[0], gla_w_gk[0], gla_b_gk[0],
                 gla_norm[0], ab_w_out[0], ffn_norm[0], ffn_w_gate[0], ffn_w_up[0], ffn_w_down[0])
    h = layer_c(h, b, tp, t_real, c_norm[0], c_w_in[0], c_w_out[0])
    out = layer_moe_final(h, moe_norm[0], moe_router[0], moe_w_gate[0], moe_w_up[0],
                          moe_w_down[0], final_norm)
    return out.reshape(b, tp, d)[:, N_META_TOK:t_real]
```

```python
import functools

import jax
import jax.numpy as jnp
import numpy as np
from jax import lax
from jax.experimental import pallas as pl
from jax.experimental.pallas import tpu as pltpu

F32 = jnp.float32
BF16 = jnp.bfloat16
I32 = jnp.int32

N_META_TOK = 16
EPS = 1e-6
ROPE_THETA = 500000.0
LRU_BLOCKS = 4
CONV_W = 4
LRU_C = 8.0
GLA_HEADS = 4
GLA_RANK = 16
GLA_GATE_NORM = 16.0
ATT_HEADS = 16
ATT_KV_HEADS = 4
ATT_GROUPS = ATT_HEADS // ATT_KV_HEADS
IDX_HEADS = 8
IDX_DIM = 64
TOPK_MAX = 256
BIG = 1e30
NEG = -1e30
N_EXPERTS = 8
LOG2_E = 1.4426950408889634

LANES = 128
SUBLANES = 8
TIME_TILE = 384
INDEX_Q_TILE = 128
ROW_TILE_BIG = 768
COL_TILE = 512
GLA_CHUNK = 64
GATHER_ROWS = 512
VMEM_LIMIT = 56 * 1024 * 1024
INT_MIN = -(2 ** 31)


def _cparams(sem):
    return pltpu.CompilerParams(dimension_semantics=sem, vmem_limit_bytes=VMEM_LIMIT)


def _rms(x, g):
    return x * lax.rsqrt(jnp.mean(x * x, axis=-1, keepdims=True) + EPS) * g


def _sigmoid(x):
    return 1.0 / (1.0 + jnp.exp(-x))


def _pick_tile(n, candidates):
    for c in candidates:
        if n % c == 0:
            return c
    raise ValueError(f"no tile in {candidates} divides {n}")


def _norm_matmul_kernel(x_ref, g_ref, w_ref, o_ref, xn_ref):
    @pl.when(pl.program_id(1) == 0)
    def _():
        xn_ref[...] = _rms(x_ref[...], g_ref[...]).astype(BF16)

    o_ref[...] = jnp.dot(xn_ref[...], w_ref[...], preferred_element_type=F32).astype(o_ref.dtype)


def norm_matmul(x, g, w):
    m, d = x.shape
    n = w.shape[1]
    tm = _pick_tile(m, (ROW_TILE_BIG, TIME_TILE))
    tn = COL_TILE
    return pl.pallas_call(
        _norm_matmul_kernel,
        out_shape=jax.ShapeDtypeStruct((m, n), BF16),
        grid=(m // tm, n // tn),
        in_specs=[pl.BlockSpec((tm, d), lambda i, j: (i, 0)),
                  pl.BlockSpec((1, d), lambda i, j: (0, 0)),
                  pl.BlockSpec((d, tn), lambda i, j: (0, j))],
        out_specs=pl.BlockSpec((tm, tn), lambda i, j: (i, j)),
        scratch_shapes=[pltpu.VMEM((tm, d), BF16)],
        compiler_params=_cparams(("parallel", "arbitrary")),
        name="norm_matmul",
    )(x, g, w)


def _matmul_res_kernel(*refs, nparts):
    xs = refs[:nparts]
    w_ref, r_ref, o_ref = refs[nparts:]
    acc = r_ref[...]
    k0 = 0
    for x_ref in xs:
        kp = x_ref.shape[1]
        acc = acc + jnp.dot(x_ref[...], w_ref[k0:k0 + kp, :], preferred_element_type=F32)
        k0 += kp
    o_ref[...] = acc


def matmul_residual(parts, w, res):
    m, n = res.shape
    k = w.shape[0]
    tm = _pick_tile(m, (ROW_TILE_BIG, TIME_TILE))
    tn = 1024
    in_specs = [pl.BlockSpec((tm, p.shape[1]), lambda i, j: (i, 0)) for p in parts]
    in_specs += [pl.BlockSpec((k, tn), lambda i, j: (0, j)),
                 pl.BlockSpec((tm, tn), lambda i, j: (i, j))]
    return pl.pallas_call(
        functools.partial(_matmul_res_kernel, nparts=len(parts)),
        out_shape=jax.ShapeDtypeStruct((m, n), F32),
        grid=(m // tm, n // tn),
        in_specs=in_specs,
        out_specs=pl.BlockSpec((tm, tn), lambda i, j: (i, j)),
        compiler_params=_cparams(("parallel", "arbitrary")),
        name="matmul_residual",
    )(*parts, w, res)


def _chunk_rows(ref, c, rows, nchunk):
    return ref[pl.ds(c, rows, stride=nchunk), :]


def _swiglu_partial(xn_ref, wg_ref, wu_ref, wd_ref):
    xn = xn_ref[...]
    gate = jnp.dot(xn, wg_ref[0], preferred_element_type=F32)
    up = jnp.dot(xn, wu_ref[0], preferred_element_type=F32)
    act = (gate * _sigmoid(gate) * up).astype(BF16)
    return jnp.dot(act, wd_ref[0], preferred_element_type=F32)


def _ffn_kernel(x_ref, g_ref, wg_ref, wu_ref, wd_ref, o_ref, xn_ref):
    j = pl.program_id(1)

    @pl.when(j == 0)
    def _():
        xn_ref[...] = _rms(x_ref[...], g_ref[...]).astype(BF16)

    part = _swiglu_partial(xn_ref, wg_ref, wu_ref, wd_ref)

    @pl.when(j == 0)
    def _():
        o_ref[...] = part + x_ref[...]

    @pl.when(j > 0)
    def _():
        o_ref[...] += part


def ffn_residual(x, g, wg, wu, wd):
    m, d = x.shape
    f = wg.shape[2]
    tm = _pick_tile(m, (ROW_TILE_BIG, TIME_TILE))
    tf = COL_TILE
    return pl.pallas_call(
        _ffn_kernel,
        out_shape=jax.ShapeDtypeStruct((m, d), F32),
        grid=(m // tm, f // tf),
        in_specs=[pl.BlockSpec((tm, d), lambda i, j: (i, 0)),
                  pl.BlockSpec((1, d), lambda i, j: (0, 0)),
                  pl.BlockSpec((1, d, tf), lambda i, j: (0, 0, j)),
                  pl.BlockSpec((1, d, tf), lambda i, j: (0, 0, j)),
                  pl.BlockSpec((1, tf, d), lambda i, j: (0, j, 0))],
        out_specs=pl.BlockSpec((tm, d), lambda i, j: (i, 0)),
        scratch_shapes=[pltpu.VMEM((tm, d), BF16)],
        compiler_params=_cparams(("parallel", "arbitrary")),
        name="ffn_residual",
    )(x, g, wg, wu, wd)


def _experts_kernel(eid_ref, nused_ref, x_ref, g_ref, wg_ref, wu_ref, wd_ref, o_ref,
                    xn_ref, acc_ref):
    i = pl.program_id(0)
    j = pl.program_id(1)
    tm, d = xn_ref.shape
    nchunk = d // LANES

    @pl.when(i < nused_ref[0])
    def _():
        @pl.when(j == 0)
        def _():
            ssq = jnp.zeros((tm, LANES), F32)
            for c in range(nchunk):
                xc = _chunk_rows(x_ref, c, tm, nchunk)
                ssq = ssq + xc * xc
            inv = lax.rsqrt(jnp.sum(ssq, axis=1, keepdims=True) / d + EPS)
            for c in range(nchunk):
                sl = slice(c * LANES, (c + 1) * LANES)
                xn_ref[:, sl] = (_chunk_rows(x_ref, c, tm, nchunk) * inv * g_ref[:, sl]).astype(BF16)

        part = _swiglu_partial(xn_ref, wg_ref, wu_ref, wd_ref)

        @pl.when(j == 0)
        def _():
            acc_ref[...] = part

        @pl.when(j > 0)
        def _():
            acc_ref[...] += part

    @pl.when(j == pl.num_programs(1) - 1)
    def _():
        @pl.when(i >= nused_ref[0])
        def _():
            acc_ref[...] = jnp.zeros_like(acc_ref)

        for c in range(nchunk):
            o_ref[pl.ds(c, tm, stride=nchunk), :] = acc_ref[:, c * LANES:(c + 1) * LANES]


def ffn_experts(x, g, wg, wu, wd, eid, nused, tm):
    d = wg.shape[1]
    f = wg.shape[2]
    nchunk = d // LANES
    nblk = x.shape[0] // (tm * nchunk)
    tf = COL_TILE
    nj = f // tf

    def jj(i, j, nused):
        return jnp.where(i < nused[0], j, nj - 1)

    grid_spec = pltpu.PrefetchScalarGridSpec(
        num_scalar_prefetch=2,
        grid=(nblk, nj),
        in_specs=[pl.BlockSpec((tm * nchunk, LANES), lambda i, j, eid, nu: (i, 0)),
                  pl.BlockSpec((1, d), lambda i, j, eid, nu: (0, 0)),
                  pl.BlockSpec((1, d, tf), lambda i, j, eid, nu: (eid[i], 0, jj(i, j, nu))),
                  pl.BlockSpec((1, d, tf), lambda i, j, eid, nu: (eid[i], 0, jj(i, j, nu))),
                  pl.BlockSpec((1, tf, d), lambda i, j, eid, nu: (eid[i], jj(i, j, nu), 0))],
        out_specs=pl.BlockSpec((tm * nchunk, LANES), lambda i, j, eid, nu: (i, 0)),
        scratch_shapes=[pltpu.VMEM((tm, d), BF16), pltpu.VMEM((tm, d), F32)],
    )
    return pl.pallas_call(
        _experts_kernel,
        out_shape=jax.ShapeDtypeStruct(x.shape, F32),
        grid_spec=grid_spec,
        compiler_params=_cparams(("parallel", "arbitrary")),
        name="ffn_experts",
    )(eid, nused, x, g, wg, wu, wd)


def _lru_kernel(xb_ref, yb_ref, cw_ref, cb_ref, wr_ref, br_ref, wi_ref, bi_ref, lam_ref,
                o_ref, xext_ref, h_ref):
    tb, w = xb_ref.shape

    @pl.when(pl.program_id(1) == 0)
    def _():
        xext_ref[0:SUBLANES, :] = jnp.zeros((SUBLANES, w), F32)
        h_ref[...] = jnp.zeros_like(h_ref)

    xext_ref[SUBLANES:SUBLANES + tb, :] = xb_ref[...].astype(F32)
    cw = cw_ref[...]
    xc = cb_ref[...] + cw[CONV_W - 1:CONV_W, :] * xext_ref[SUBLANES:SUBLANES + tb, :]
    for j in range(CONV_W - 1):
        xc = xc + cw[j:j + 1, :] * xext_ref[pl.ds(SUBLANES - (CONV_W - 1) + j, tb), :]
    xext_ref[0:SUBLANES, :] = xext_ref[tb:tb + SUBLANES, :]

    xcb = xc.astype(BF16)
    nb = wr_ref.shape[0]
    bw = w // nb
    r_parts, i_parts = [], []
    for n in range(nb):
        xs = xcb[:, n * bw:(n + 1) * bw]
        r_parts.append(jnp.dot(xs, wr_ref[n], preferred_element_type=F32))
        i_parts.append(jnp.dot(xs, wi_ref[n], preferred_element_type=F32))
    r = _sigmoid(jnp.concatenate(r_parts, axis=1) + br_ref[...])
    ig = _sigmoid(jnp.concatenate(i_parts, axis=1) + bi_ref[...])

    neg_lam = -lam_ref[...]
    softplus = jnp.maximum(neg_lam, 0.0) + jnp.log1p(jnp.exp(-jnp.abs(neg_lam)))
    log_a = -LRU_C * r * softplus
    a = jnp.exp(log_a)
    u = jnp.sqrt(1.0 - a * a) * (ig * xc)

    row = lax.broadcasted_iota(I32, (tb, w), 0)
    s = 1
    while s < tb:
        keep = row >= s
        a_sh = pltpu.roll(a, s, 0)
        u_sh = pltpu.roll(u, s, 0)
        u = jnp.where(keep, a * u_sh + u, u)
        a = jnp.where(keep, a * a_sh, a)
        s *= 2
    h = a * h_ref[0:1, :] + u
    h_ref[0:1, :] = h[tb - 1:tb, :]

    y = yb_ref[...].astype(F32)
    gelu = 0.5 * y * (1.0 + jnp.tanh(0.7978845608028654 * (y + 0.044715 * (y * y * y))))
    o_ref[...] = (h * gelu).astype(o_ref.dtype)


def rg_lru(z, conv_w, conv_b, w_r, b_r, w_i, b_i, lam):
    b, tp, _ = z.shape
    w = conv_w.shape[1]
    tb = TIME_TILE
    nb = w_r.shape[0]
    bw = w // nb
    full = lambda shape: pl.BlockSpec(shape, lambda bi, ti: (0,) * len(shape))
    return pl.pallas_call(
        _lru_kernel,
        out_shape=jax.ShapeDtypeStruct((b, tp, w), BF16),
        grid=(b, tp // tb),
        in_specs=[pl.BlockSpec((None, tb, w), lambda bi, ti: (bi, ti, 0)),
                  pl.BlockSpec((None, tb, w), lambda bi, ti: (bi, ti, 1)),
                  full((CONV_W, w)), full((1, w)),
                  full((nb, bw, bw)), full((1, w)),
                  full((nb, bw, bw)), full((1, w)),
                  full((1, w))],
        out_specs=pl.BlockSpec((None, tb, w), lambda bi, ti: (bi, ti, 0)),
        scratch_shapes=[pltpu.VMEM((tb + 2 * SUBLANES, w), F32), pltpu.VMEM((SUBLANES, w), F32)],
        compiler_params=_cparams(("parallel", "arbitrary")),
        name="rg_lru",
    )(z, z, conv_w, conv_b, w_r, b_r, w_i, b_i, lam)


def _gla_kernel(q_ref, k_ref, v_ref, g_ref, glr_ref, wgk_ref, bgk_ref, ng_ref, o_ref, st_ref,
                *, chunk, heads):
    tb, hdk = q_ref.shape
    dk = hdk // heads
    dv = v_ref.shape[1] // heads
    c = chunk

    @pl.when(pl.program_id(1) == 0)
    def _():
        st_ref[...] = jnp.zeros_like(st_ref)

    row = lax.broadcasted_iota(I32, (c, hdk), 0)
    tri = lax.broadcasted_iota(I32, (c, c), 0) >= lax.broadcasted_iota(I32, (c, c), 1)
    nt = (((1,), (1,)), ((), ()))
    tn = (((0,), (0,)), ((), ()))

    def body(ci, carry):
        sl = pl.ds(pl.multiple_of(ci * c, c), c)
        pre = jnp.dot(glr_ref[sl, :], wgk_ref[...], preferred_element_type=F32) + bgk_ref[...]
        gk = (jnp.minimum(pre, 0.0) - jnp.log1p(jnp.exp(-jnp.abs(pre)))) * (1.0 / GLA_GATE_NORM)
        bc = gk
        s = 1
        while s < c:
            bc = bc + jnp.where(row >= s, pltpu.roll(bc, s, 0), 0.0)
            s *= 2
        q = q_ref[sl, :].astype(F32) * (dk ** -0.5)
        k = k_ref[sl, :].astype(F32)
        v = v_ref[sl, :]
        g = g_ref[sl, :].astype(F32)
        bmid = bc[c // 2:c // 2 + 1, :]
        blast = bc[c - 1:c, :]
        q_inter = (q * jnp.exp(bc)).astype(BF16)
        q_intra = (q * jnp.exp(bc - bmid)).astype(BF16)
        k_intra = (k * jnp.exp(bmid - bc)).astype(BF16)
        k_state = (k * jnp.exp(blast - bc)).astype(BF16)
        decay = jnp.exp(blast)
        outs = []
        for h in range(heads):
            ks = slice(h * dk, (h + 1) * dk)
            vs = slice(h * dv, (h + 1) * dv)
            st = st_ref[h]
            vh = v[:, vs]
            inter = lax.dot_general(q_inter[:, ks], st.astype(BF16), nt, preferred_element_type=F32)
            att = lax.dot_general(q_intra[:, ks], k_intra[:, ks], nt, preferred_element_type=F32)
            att = jnp.where(tri, att, 0.0).astype(BF16)
            o = inter + jnp.dot(att, vh, preferred_element_type=F32)
            st_ref[h] = st * decay[:, ks] + lax.dot_general(vh, k_state[:, ks], tn,
                                                           preferred_element_type=F32)
            o = _rms(o, ng_ref[...])
            gh = g[:, vs]
            outs.append(o * (gh * _sigmoid(gh)))
        o_ref[sl, :] = jnp.concatenate(outs, axis=1).astype(o_ref.dtype)
        return carry

    lax.fori_loop(0, tb // c, body, 0)


def gla(z, col0, w_gk, b_gk, norm_g, hdk, hdv):
    b, tp, _ = z.shape
    tb = TIME_TILE
    cq, ck = col0 // hdk, col0 // hdk + 1
    cv = (col0 + 2 * hdk) // hdv
    cg = cv + 1
    cl = (col0 + 2 * hdk + 2 * hdv) // LANES
    full = lambda shape: pl.BlockSpec(shape, lambda bi, ti: (0,) * len(shape))
    return pl.pallas_call(
        functools.partial(_gla_kernel, chunk=GLA_CHUNK, heads=GLA_HEADS),
        out_shape=jax.ShapeDtypeStruct((b, tp, hdv), BF16),
        grid=(b, tp // tb),
        in_specs=[pl.BlockSpec((None, tb, hdk), lambda bi, ti: (bi, ti, cq)),
                  pl.BlockSpec((None, tb, hdk), lambda bi, ti: (bi, ti, ck)),
                  pl.BlockSpec((None, tb, hdv), lambda bi, ti: (bi, ti, cv)),
                  pl.BlockSpec((None, tb, hdv), lambda bi, ti: (bi, ti, cg)),
                  pl.BlockSpec((None, tb, LANES), lambda bi, ti: (bi, ti, cl)),
                  full((LANES, hdk)), full((1, hdk)), full((1, hdv // GLA_HEADS))],
        out_specs=pl.BlockSpec((None, tb, hdv), lambda bi, ti: (bi, ti, 0)),
        scratch_shapes=[pltpu.VMEM((GLA_HEADS, hdv // GLA_HEADS, hdk // GLA_HEADS), F32)],
        compiler_params=_cparams(("parallel", "arbitrary")),
        name="gla",
    )(z, z, z, z, z, w_gk, b_gk, norm_g)


def _rope_tile(x, cos, sin_up, sin_dn, half):
    return x * cos + pltpu.roll(x, half, 1) * sin_up + pltpu.roll(x, LANES - half, 1) * sin_dn


def _rope_kernel(qk_ref, qi_ref, kw_ref, tab_ref, qk_out, qi_out, ki_out, wi_out):
    def tabs(kind):
        return tab_ref[3 * kind], tab_ref[3 * kind + 1], tab_ref[3 * kind + 2]

    n_q_tiles = ATT_HEADS
    for t in range(qk_ref.shape[1] // LANES):
        ca, sa_up, sa_dn = tabs(0 if t < n_q_tiles else 1)
        sl = slice(t * LANES, (t + 1) * LANES)
        qk_out[:, sl] = _rope_tile(qk_ref[:, sl].astype(F32), ca, sa_up, sa_dn, 16).astype(BF16)
    cb, sb_up, sb_dn = tabs(2)
    for t in range(qi_ref.shape[1] // LANES):
        sl = slice(t * LANES, (t + 1) * LANES)
        qi_out[:, sl] = _rope_tile(qi_ref[:, sl].astype(F32), cb, sb_up, sb_dn, 8).astype(BF16)
    ck, sk_up, sk_dn = tabs(3)
    kw = _rope_tile(kw_ref[...].astype(F32), ck, sk_up, sk_dn, 8)
    ki_out[...] = kw.astype(BF16)
    wi_out[...] = pltpu.roll(kw, LANES - IDX_DIM, 1)


def rope_tables(tp, head_dim):
    pos = jnp.arange(tp, dtype=F32)[:, None]
    lane = jnp.arange(LANES)

    def tables(dim, period, active):
        half = (dim // 4) // 2
        freqs = ROPE_THETA ** (-jnp.arange(half, dtype=F32) / half)
        ang = pos * freqs[None, :]
        cos, sin = jnp.cos(ang), jnp.sin(ang)
        within = lane % period
        fidx = within % half
        cos_l, sin_l = cos[:, fidx], sin[:, fidx]
        lo = (within < half) & active
        hi = (within >= half) & (within < 2 * half) & active
        c = jnp.where((lo | hi)[None, :], cos_l, 1.0)
        s_up = jnp.where(hi[None, :], sin_l, 0.0)
        s_dn = jnp.where(lo[None, :], -sin_l, 0.0)
        return c, s_up, s_dn

    all_on = jnp.ones((LANES,), bool)
    a = tables(head_dim, head_dim, all_on)
    bq = tables(IDX_DIM, IDX_DIM, all_on)
    ck, sk_up, sk_dn = tables(IDX_DIM, IDX_DIM, lane < IDX_DIM)
    wscale = IDX_HEADS ** -0.5 * IDX_DIM ** -0.5
    ck = jnp.where((lane >= IDX_DIM)[None, :],
                   jnp.where(lane < IDX_DIM + IDX_HEADS, wscale, 0.0)[None, :], ck)
    a_scaled = [t * (head_dim ** -0.5 * LOG2_E) for t in a]
    return jnp.stack([*a_scaled, *a, *bq, ck, sk_up, sk_dn])


def rope(z, tabs, tp, nqk, nqi, col_qi, col_kw):
    m = z.shape[0]
    tm = TIME_TILE
    nper = tp // tm
    return pl.pallas_call(
        _rope_kernel,
        out_shape=[jax.ShapeDtypeStruct((m, nqk), BF16), jax.ShapeDtypeStruct((m, nqi), BF16),
                   jax.ShapeDtypeStruct((m, LANES), BF16), jax.ShapeDtypeStruct((m, LANES), F32)],
        grid=(m // tm,),
        in_specs=[pl.BlockSpec((tm, nqk), lambda i: (i, 0)),
                  pl.BlockSpec((tm, nqi), lambda i: (i, col_qi // nqi)),
                  pl.BlockSpec((tm, LANES), lambda i: (i, col_kw // LANES)),
                  pl.BlockSpec((12, tm, LANES), lambda i: (0, i % nper, 0))],
        out_specs=[pl.BlockSpec((tm, nqk), lambda i: (i, 0)),
                   pl.BlockSpec((tm, nqi), lambda i: (i, 0)),
                   pl.BlockSpec((tm, LANES), lambda i: (i, 0)),
                   pl.BlockSpec((tm, LANES), lambda i: (i, 0))],
        compiler_params=_cparams(("parallel",)),
        name="rope",
    )(z, z, z, tabs)


def _index_keys(qi_heads, w_cols, ki_blk, q0, k0):
    nt = (((1,), (1,)), ((), ()))
    score = None
    for qh, wh in zip(qi_heads, w_cols):
        lg = lax.dot_general(qh, ki_blk, nt, preferred_element_type=F32)
        term = wh * jnp.maximum(lg, 0.0)
        score = term if score is None else score + term
    tq, tk = score.shape
    qpos = q0 + lax.broadcasted_iota(I32, (tq, tk), 0)
    kpos = k0 + lax.broadcasted_iota(I32, (tq, tk), 1)
    score = jnp.where(kpos <= qpos, jnp.where(kpos < N_META_TOK, BIG, score), NEG)
    bits = pltpu.bitcast(score, I32)
    keys = jnp.where(bits < 0, bits ^ jnp.int32(0x7FFFFFFF), bits)
    return keys, kpos, score


def _split_index_heads(qi, wi):
    qi_heads = [qi[:, h * IDX_DIM:(h + 1) * IDX_DIM] for h in range(IDX_HEADS)]
    w_cols = [wi[:, h:h + 1] for h in range(IDX_HEADS)]
    return qi_heads, w_cols


def _float_key(x):
    bits = int(np.float32(x).view(np.int32))
    return bits ^ 0x7FFFFFFF if bits < 0 else bits


def _indexer_kernel(qi_ref, wi_ref, ki_ref, bias_ref, keys_ref, cut_ref, *, k_sel, tk):
    tq = qi_ref.shape[0]
    qb = pl.program_id(1)
    q0 = qb * tq
    nkv = (q0 + tq + tk - 1) // tk
    qi_heads, w_cols = _split_index_heads(qi_ref[...], wi_ref[...])

    def fill(kb, carry):
        k0 = pl.multiple_of(kb * tk, tk)
        keys, _, _ = _index_keys(qi_heads, w_cols, ki_ref[pl.ds(k0, tk), 0:IDX_DIM], q0, k0)
        keys_ref[:, pl.ds(k0, tk)] = keys
        return carry

    lax.fori_loop(0, nkv, fill, 0)

    def count(pred):
        def blk(kb, acc):
            k0 = pl.multiple_of(kb * tk, tk)
            keys = keys_ref[:, pl.ds(k0, tk)]
            kpos = k0 + lax.broadcasted_iota(I32, (tq, tk), 1)
            hit = jnp.where(pred(keys, kpos), 1, 0)
            for t in range(tk // LANES):
                acc = acc + hit[:, t * LANES:(t + 1) * LANES]
            return acc
        acc = lax.fori_loop(0, nkv, blk, jnp.zeros((tq, LANES), I32))
        return jnp.sum(acc, axis=1, keepdims=True)

    def bit_step(bi, thr):
        cand = thr + jnp.left_shift(jnp.int32(1), 31 - bi)
        n = count(lambda keys, kpos: keys >= cand)
        return jnp.where(n >= k_sel, cand, thr)

    thr = lax.fori_loop(0, 32, bit_step, jnp.full((tq, 1), INT_MIN, I32))
    n_ge = count(lambda keys, kpos: keys >= thr)
    no_tie_cut = jnp.full((tq, 1), 1 << 30, I32)
    cut_ref[...] = no_tie_cut

    @pl.when(jnp.max(n_ge) > k_sel)
    def _():
        n_gt = count(lambda keys, kpos: keys > thr)
        need = k_sel - n_gt

        def cut_step(bi, cut):
            cand = cut + jnp.left_shift(jnp.int32(1), 15 - bi)
            n = count(lambda keys, kpos: (keys == thr) & (kpos < cand))
            return jnp.where(n <= need, cand, cut)

        cut = lax.fori_loop(0, 16, cut_step, jnp.zeros((tq, 1), I32))
        cut_ref[...] = jnp.where(n_ge > k_sel, cut, no_tie_cut)

    bias_ref[...] = jnp.full(bias_ref.shape, NEG, BF16)
    cut = cut_ref[...]
    key_valid = jnp.int32(_float_key(0.5 * NEG))

    def emit(kb, carry):
        k0 = pl.multiple_of(kb * tk, tk)
        keys = keys_ref[:, pl.ds(k0, tk)]
        kpos = k0 + lax.broadcasted_iota(I32, (tq, tk), 1)
        chosen = ((keys > thr) | ((keys == thr) & (kpos < cut))) & (keys > key_valid)
        bias_ref[:, pl.ds(k0, tk)] = jnp.where(chosen, 0.0, NEG).astype(BF16)
        return carry

    lax.fori_loop(0, nkv, emit, 0)


def indexer(qi, wi, ki, b, tp, k_sel):
    tq = INDEX_Q_TILE
    nq = tp // tq
    return pl.pallas_call(
        functools.partial(_indexer_kernel, k_sel=k_sel, tk=TIME_TILE),
        out_shape=jax.ShapeDtypeStruct((b * tp, tp), BF16),
        grid=(b, nq),
        in_specs=[pl.BlockSpec((tq, qi.shape[1]), lambda bi, qb: (bi * nq + qb, 0)),
                  pl.BlockSpec((tq, LANES), lambda bi, qb: (bi * nq + qb, 0)),
                  pl.BlockSpec((tp, LANES), lambda bi, qb: (bi, 0))],
        out_specs=pl.BlockSpec((tq, tp), lambda bi, qb: (bi * nq + qb, 0)),
        scratch_shapes=[pltpu.VMEM((tq, tp), I32), pltpu.VMEM((tq, 1), I32)],
        compiler_params=_cparams(("parallel", "arbitrary")),
        name="indexer",
    )(qi, wi, ki)


def _attn_kernel(q_ref, k_ref, v_ref, bias_ref, o_ref, qs_ref, m_ref, l_ref, acc_ref):
    tq = q_ref.shape[0]
    tk = k_ref.shape[0]
    hd = k_ref.shape[1] // ATT_KV_HEADS
    grows = ATT_GROUPS * tq
    qb = pl.program_id(1)
    kb = pl.program_id(2)
    nt = (((1,), (1,)), ((), ()))

    @pl.when(kb == 0)
    def _():
        m_ref[...] = jnp.full_like(m_ref, NEG)
        l_ref[...] = jnp.zeros_like(l_ref)
        acc_ref[...] = jnp.zeros_like(acc_ref)
        for hi in range(ATT_HEADS):
            qs_ref[hi * tq:(hi + 1) * tq, :] = q_ref[:, hi * hd:(hi + 1) * hd]

    @pl.when(kb <= qb)
    def _():
        bias = bias_ref[...].astype(F32)[None]
        ones = jnp.ones((tk, LANES), BF16)
        for kvh in range(ATT_KV_HEADS):
            rows = slice(kvh * grows, (kvh + 1) * grows)
            kh = k_ref[:, kvh * hd:(kvh + 1) * hd]
            vh1 = jnp.concatenate([v_ref[:, kvh * hd:(kvh + 1) * hd], ones], axis=1)
            s = lax.dot_general(qs_ref[rows, :], kh, nt, preferred_element_type=F32)
            s = (s.reshape(ATT_GROUPS, tq, tk) + bias).reshape(grows, tk)
            m_old = m_ref[rows, :]
            m_new = jnp.maximum(m_old, jnp.broadcast_to(jnp.max(s, axis=1, keepdims=True),
                                                        (grows, LANES)))
            p = jnp.concatenate(
                [jnp.exp2(s[:, t * LANES:(t + 1) * LANES] - m_new).astype(BF16)
                 for t in range(tk // LANES)], axis=1)
            alpha = jnp.exp2(m_old - m_new)
            pv = jnp.dot(p, vh1, preferred_element_type=F32)
            l_ref[rows, :] = alpha * l_ref[rows, :] + pv[:, hd:]
            acc_ref[rows, :] = alpha * acc_ref[rows, :] + pv[:, :hd]
            m_ref[rows, :] = m_new

    @pl.when(kb == pl.num_programs(2) - 1)
    def _():
        for hi in range(ATT_HEADS):
            rows = slice(hi * tq, (hi + 1) * tq)
            o_ref[:, hi * hd:(hi + 1) * hd] = (acc_ref[rows, :] / l_ref[rows, :]).astype(o_ref.dtype)


def sparse_attention(qk, z, bias, b, tp, d, col_v):
    tq = tk = TIME_TILE
    nq = tp // tq
    hd = d // ATT_HEADS
    kvw = ATT_KV_HEADS * hd
    kvrow = lambda bi, qb, kb: bi * nq + jnp.minimum(kb, qb)
    qrow = lambda bi, qb, kb: bi * nq + qb
    return pl.pallas_call(
        _attn_kernel,
        out_shape=jax.ShapeDtypeStruct((b * tp, d), BF16),
        grid=(b, nq, nq),
        in_specs=[pl.BlockSpec((tq, d), lambda bi, qb, kb: (qrow(bi, qb, kb), 0)),
                  pl.BlockSpec((tk, kvw), lambda bi, qb, kb: (kvrow(bi, qb, kb), d // kvw)),
                  pl.BlockSpec((tk, kvw), lambda bi, qb, kb: (kvrow(bi, qb, kb), col_v // kvw)),
                  pl.BlockSpec((tq, tk), lambda bi, qb, kb: (qrow(bi, qb, kb), jnp.minimum(kb, qb)))],
        out_specs=pl.BlockSpec((tq, d), lambda bi, qb, kb: (qrow(bi, qb, kb), 0)),
        scratch_shapes=[pltpu.VMEM((ATT_HEADS * tq, hd), BF16),
                        pltpu.VMEM((ATT_HEADS * tq, LANES), F32),
                        pltpu.VMEM((ATT_HEADS * tq, LANES), F32),
                        pltpu.VMEM((ATT_HEADS * tq, hd), F32)],
        compiler_params=_cparams(("parallel", "parallel", "arbitrary")),
        name="sparse_attention",
    )(qk, qk, z, bias)


def _router_kernel(x_ref, g_ref, w_ref, o_ref, xt_ref):
    tm, d = x_ref.shape
    nchunk = d // LANES
    for c in range(nchunk):
        xt_ref[pl.ds(c, tm, stride=nchunk), :] = x_ref[:, c * LANES:(c + 1) * LANES]
    xn = _rms(x_ref[...], g_ref[...]).astype(BF16)
    logits = jnp.dot(xn, w_ref[...], preferred_element_type=F32)
    lane = lax.broadcasted_iota(I32, logits.shape, 1)
    logits = jnp.where(lane < N_EXPERTS, logits, -jnp.inf)
    v1 = jnp.max(logits, axis=1, keepdims=True)
    e1 = jnp.min(jnp.where(logits == v1, lane, LANES), axis=1, keepdims=True)
    rest = jnp.where(lane == e1, -jnp.inf, logits)
    v2 = jnp.max(rest, axis=1, keepdims=True)
    e2 = jnp.min(jnp.where(rest == v2, lane, LANES), axis=1, keepdims=True)
    ex = jnp.exp(v2 - v1)
    g1 = 1.0 / (1.0 + ex)
    g2 = ex / (1.0 + ex)
    out = jnp.where(lane == 0, g1, jnp.where(lane == 1, g2, 0.0))
    out = jnp.where(lane == 2, e1.astype(F32), jnp.where(lane == 3, e2.astype(F32), out))
    o_ref[...] = out


def router(x, g, w):
    m, d = x.shape
    tm = TIME_TILE
    nchunk = d // LANES
    return pl.pallas_call(
        _router_kernel,
        out_shape=[jax.ShapeDtypeStruct((m, LANES), F32),
                   jax.ShapeDtypeStruct((m * nchunk, LANES), F32)],
        grid=(m // tm,),
        in_specs=[pl.BlockSpec((tm, d), lambda i: (i, 0)),
                  pl.BlockSpec((1, d), lambda i: (0, 0)),
                  pl.BlockSpec((d, LANES), lambda i: (0, 0))],
        out_specs=[pl.BlockSpec((tm, LANES), lambda i: (i, 0)),
                   pl.BlockSpec((tm * nchunk, LANES), lambda i: (i, 0))],
        compiler_params=_cparams(("parallel",)),
        name="router",
    )(x, g, w)


def _gather_kernel(idx_ref, src_ref, dst_ref, sem, *, rows, nchunk):
    base = pl.program_id(0) * rows

    def token_copy(r):
        src_row = pl.multiple_of(idx_ref[base + r] * nchunk, nchunk)
        dst_row = pl.multiple_of((base + r) * nchunk, nchunk)
        return pltpu.make_async_copy(src_ref.at[pl.ds(src_row, nchunk)],
                                     dst_ref.at[pl.ds(dst_row, nchunk)], sem)

    def start(r, carry):
        token_copy(r).start()
        return carry

    def wait(r, carry):
        token_copy(r).wait()
        return carry

    lax.fori_loop(0, rows, start, 0)
    lax.fori_loop(0, rows, wait, 0)


def gather_tokens(src, idx, nchunk):
    n = idx.shape[0]
    rows = _pick_tile(n, (GATHER_ROWS, TIME_TILE, 256, 128))
    grid_spec = pltpu.PrefetchScalarGridSpec(
        num_scalar_prefetch=1,
        grid=(n // rows,),
        in_specs=[pl.BlockSpec(memory_space=pl.ANY)],
        out_specs=pl.BlockSpec(memory_space=pl.ANY),
        scratch_shapes=[pltpu.SemaphoreType.DMA(())],
    )
    return pl.pallas_call(
        functools.partial(_gather_kernel, rows=rows, nchunk=nchunk),
        out_shape=jax.ShapeDtypeStruct((n * nchunk, LANES), src.dtype),
        grid_spec=grid_spec,
        compiler_params=_cparams(("arbitrary",)),
        name="gather_tokens",
    )(idx, src)


def _combine_norm_kernel(h_ref, y1_ref, y2_ref, route_ref, g_ref, o_ref):
    tm, d = h_ref.shape
    nchunk = d // LANES
    g1 = route_ref[:, 0:1]
    g2 = route_ref[:, 1:2]
    ssq = jnp.zeros((tm, LANES), F32)
    for c in range(nchunk):
        sl = slice(c * LANES, (c + 1) * LANES)
        hc = (h_ref[:, sl] + g1 * _chunk_rows(y1_ref, c, tm, nchunk)
              + g2 * _chunk_rows(y2_ref, c, tm, nchunk))
        o_ref[:, sl] = hc
        ssq = ssq + hc * hc
    inv = lax.rsqrt(jnp.sum(ssq, axis=1, keepdims=True) / d + EPS)
    o_ref[...] = o_ref[...] * inv * g_ref[...]


def combine_norm(h, y, route, g):
    m, d = h.shape
    tm = TIME_TILE
    nblk = m // tm
    nchunk = d // LANES
    return pl.pallas_call(
        _combine_norm_kernel,
        out_shape=jax.ShapeDtypeStruct((m, d), F32),
        grid=(nblk,),
        in_specs=[pl.BlockSpec((tm, d), lambda i: (i, 0)),
                  pl.BlockSpec((tm * nchunk, LANES), lambda i: (i, 0)),
                  pl.BlockSpec((tm * nchunk, LANES), lambda i: (i + nblk, 0)),
                  pl.BlockSpec((tm, LANES), lambda i: (i, 0)),
                  pl.BlockSpec((1, d), lambda i: (0, 0))],
        out_specs=pl.BlockSpec((tm, d), lambda i: (i, 0)),
        compiler_params=_cparams(("parallel",)),
        name="combine_norm",
    )(h, y, y, route, g)


def _pad_cols(w, n):
    return jnp.pad(w, ((0, 0), (0, n - w.shape[1])))


def _round_up(n, k):
    return -(-n // k) * k


def layer_ab(h, b, tp, norm_g, w_in, conv_w, conv_b, w_r, b_r, w_i, b_i, lam, w_gk, b_gk,
             gla_norm, w_out, ffn_norm, ffn_wg, ffn_wu, ffn_wd):
    m, d = h.shape
    lw = conv_w.shape[1]
    hdk = w_gk.shape[1]
    hdv = d - lw
    n_in = _round_up(w_in.shape[1] - GLA_RANK + LANES, COL_TILE)
    z = norm_matmul(h, norm_g[None], _pad_cols(w_in, n_in).astype(BF16)).reshape(b, tp, n_in)
    a_out = rg_lru(z, conv_w, conv_b[None], w_r.astype(BF16), b_r.reshape(1, lw),
                   w_i.astype(BF16), b_i.reshape(1, lw), lam[None])
    w_gk_p = jnp.pad(w_gk, ((0, LANES - GLA_RANK), (0, 0))).astype(BF16)
    b_out = gla(z, 2 * lw, w_gk_p, b_gk[None], gla_norm[None], hdk, hdv)
    h = matmul_residual([a_out.reshape(m, lw), b_out.reshape(m, hdv)], w_out.astype(BF16), h)
    return ffn_residual(h, ffn_norm[None], ffn_wg.astype(BF16)[None], ffn_wu.astype(BF16)[None],
                        ffn_wd.astype(BF16)[None])


def layer_c(h, b, tp, t_real, norm_g, w_in, w_out):
    m, d = h.shape
    hd = d // ATT_HEADS
    kvw = ATT_KV_HEADS * hd
    nqi = IDX_HEADS * IDX_DIM
    col_v = d + kvw
    col_qi = d + 2 * kvw
    col_kw = col_qi + nqi
    n_in = _round_up(col_kw + LANES, COL_TILE)
    z = norm_matmul(h, norm_g[None], _pad_cols(w_in, n_in).astype(BF16))
    qk, qi, ki, wi = rope(z, rope_tables(tp, hd), tp, d + kvw, nqi, col_qi, col_kw)
    k_sel = min(TOPK_MAX, (t_real - N_META_TOK) // 4)
    bias = indexer(qi, wi, ki, b, tp, k_sel)
    att = sparse_attention(qk, z, bias, b, tp, d, col_v)
    return matmul_residual([att], w_out.astype(BF16), h)


def moe_dispatch(route, tm):
    m = route.shape[0]
    e_flat = route[:, 2:4].astype(I32).reshape(-1)
    onehot = (e_flat[:, None] == jnp.arange(N_EXPERTS)[None, :]).astype(I32)
    rank = jnp.take_along_axis(jnp.cumsum(onehot, axis=0) - onehot, e_flat[:, None], axis=1)[:, 0]
    counts = jnp.sum(onehot, axis=0)
    padded = (counts + tm - 1) // tm * tm
    pad_end = jnp.cumsum(padded)
    dest = (pad_end - padded)[e_flat] + rank
    nblk = -(-(2 * m) // tm) + N_EXPERTS
    slot_tok = jnp.zeros((nblk * tm,), I32).at[dest].set(jnp.arange(2 * m, dtype=I32) // 2)
    nused = (pad_end[-1] // tm).astype(I32)
    blk = jnp.minimum(jnp.arange(nblk, dtype=I32), nused - 1)
    blk_expert = jnp.minimum(jnp.sum((pad_end[None, :] <= (blk * tm)[:, None]).astype(I32), axis=1),
                             N_EXPERTS - 1)
    return slot_tok, dest.reshape(m, 2), blk_expert, nused.reshape(1)


def layer_moe_final(h, norm_g, w_router, wg, wu, wd, final_g):
    m, d = h.shape
    nchunk = d // LANES
    route, h_tok = router(h, norm_g[None], _pad_cols(w_router, LANES).astype(BF16))
    tm = _pick_tile(m, (ROW_TILE_BIG, TIME_TILE))
    slot_tok, tok_slots, blk_expert, nused = moe_dispatch(route, tm)
    x_slots = gather_tokens(h_tok, slot_tok, nchunk)
    y_slots = ffn_experts(x_slots, norm_g[None], wg.astype(BF16), wu.astype(BF16), wd.astype(BF16),
                          blk_expert, nused, tm)
    y_tok = gather_tokens(y_slots, tok_slots.T.reshape(-1), nchunk)
    return combine_norm(h, y_tok, route, final_g[None])


def kernel(x, meta, ab_norm, ab_w_in, lru_conv_w, lru_conv_b, lru_w_r, lru_b_r, lru_w_i, lru_b_i,
           lru_lam, gla_w_gk, gla_b_gk, gla_norm, ab_w_out, ffn_norm, ffn_w_gate, ffn_w_up,
           ffn_w_down, c_norm, c_w_in, c_w_out, moe_norm, moe_router, moe_w_gate, moe_w_up,
           moe_w_down, final_norm):
    b, seq, d = x.shape
    assert ab_norm.shape[0] == 1 and c_norm.shape[0] == 1, "two-layer trunk only"
    t_real = seq + N_META_TOK
    tp = _round_up(t_real, TIME_TILE)
    h = jnp.concatenate([jnp.broadcast_to(meta.astype(x.dtype)[None], (b, N_META_TOK, d)), x,
                         jnp.zeros((b, tp - t_real, d), x.dtype)], axis=1).reshape(b * tp, d)
    h = layer_ab(h, b, tp, ab_norm[0], ab_w_in[0], lru_conv_w[0], lru_conv_b[0], lru_w_r[0],
                 lru_b_r[0], lru_w_i[0], lru_b_i[0], lru_lam[0], gla_w_gk[0], gla_b_gk[0],
                 gla_norm[0], ab_w_out[0], ffn_norm[0], ffn_w_gate[0], ffn_w_up[0], ffn_w_down[0])
    h = layer_c(h, b, tp, t_real, c_norm[0], c_w_in[0], c_w_out[0])
    out = layer_moe_final(h, moe_norm[0], moe_router[0], moe_w_gate[0], moe_w_up[0],
                          moe_w_down[0], final_norm)
    return out.reshape(b, tp, d)[:, N_META_TOK:t_real]
```

```python
import functools

import jax
import jax.numpy as jnp
import numpy as np
from jax import lax
from jax.experimental import pallas as pl
from jax.experimental.pallas import tpu as pltpu

F32 = jnp.float32
BF16 = jnp.bfloat16
I32 = jnp.int32

N_META_TOK = 16
EPS = 1e-6
ROPE_THETA = 500000.0
LRU_BLOCKS = 4
CONV_W = 4
LRU_C = 8.0
GLA_HEADS = 4
GLA_RANK = 16
GLA_GATE_NORM = 16.0
ATT_HEADS = 16
ATT_KV_HEADS = 4
ATT_GROUPS = ATT_HEADS // ATT_KV_HEADS
IDX_HEADS = 8
IDX_DIM = 64
TOPK_MAX = 256
BIG = 1e30
NEG = -1e30
N_EXPERTS = 8
LOG2_E = 1.4426950408889634

LANES = 128
SUBLANES = 8
TIME_TILE = 384
INDEX_Q_TILE = 128
ROW_TILE_BIG = 768
COL_TILE = 512
GLA_CHUNK = 64
GATHER_ROWS = 512
VMEM_LIMIT = 56 * 1024 * 1024
INT_MIN = -(2 ** 31)


def _cparams(sem):
    return pltpu.CompilerParams(dimension_semantics=sem, vmem_limit_bytes=VMEM_LIMIT)


def _rms(x, g):
    return x * lax.rsqrt(jnp.mean(x * x, axis=-1, keepdims=True) + EPS) * g


def _sigmoid(x):
    return 1.0 / (1.0 + jnp.exp(-x))


def _pick_tile(n, candidates):
    for c in candidates:
        if n % c == 0:
            return c
    raise ValueError(f"no tile in {candidates} divides {n}")


def _norm_matmul_kernel(x_ref, g_ref, w_ref, o_ref, xn_ref):
    @pl.when(pl.program_id(1) == 0)
    def _():
        xn_ref[...] = _rms(x_ref[...], g_ref[...]).astype(BF16)

    o_ref[...] = jnp.dot(xn_ref[...], w_ref[...], preferred_element_type=F32).astype(o_ref.dtype)


def norm_matmul(x, g, w):
    m, d = x.shape
    n = w.shape[1]
    tm = _pick_tile(m, (ROW_TILE_BIG, TIME_TILE))
    tn = COL_TILE
    return pl.pallas_call(
        _norm_matmul_kernel,
        out_shape=jax.ShapeDtypeStruct((m, n), BF16),
        grid=(m // tm, n // tn),
        in_specs=[pl.BlockSpec((tm, d), lambda i, j: (i, 0)),
                  pl.BlockSpec((1, d), lambda i, j: (0, 0)),
                  pl.BlockSpec((d, tn), lambda i, j: (0, j))],
        out_specs=pl.BlockSpec((tm, tn), lambda i, j: (i, j)),
        scratch_shapes=[pltpu.VMEM((tm, d), BF16)],
        compiler_params=_cparams(("parallel", "arbitrary")),
        name="norm_matmul",
    )(x, g, w)


def _matmul_res_kernel(*refs, nparts):
    xs = refs[:nparts]
    w_ref, r_ref, o_ref = refs[nparts:]
    acc = r_ref[...]
    k0 = 0
    for x_ref in xs:
        kp = x_ref.shape[1]
        acc = acc + jnp.dot(x_ref[...], w_ref[k0:k0 + kp, :], preferred_element_type=F32)
        k0 += kp
    o_ref[...] = acc


def matmul_residual(parts, w, res):
    m, n = res.shape
    k = w.shape[0]
    tm = _pick_tile(m, (ROW_TILE_BIG, TIME_TILE))
    tn = 1024
    in_specs = [pl.BlockSpec((tm, p.shape[1]), lambda i, j: (i, 0)) for p in parts]
    in_specs += [pl.BlockSpec((k, tn), lambda i, j: (0, j)),
                 pl.BlockSpec((tm, tn), lambda i, j: (i, j))]
    return pl.pallas_call(
        functools.partial(_matmul_res_kernel, nparts=len(parts)),
        out_shape=jax.ShapeDtypeStruct((m, n), F32),
        grid=(m // tm, n // tn),
        in_specs=in_specs,
        out_specs=pl.BlockSpec((tm, tn), lambda i, j: (i, j)),
        compiler_params=_cparams(("parallel", "arbitrary")),
        name="matmul_residual",
    )(*parts, w, res)


def _chunk_rows(ref, c, rows, nchunk):
    return ref[pl.ds(c, rows, stride=nchunk), :]


def _swiglu_partial(xn_ref, wg_ref, wu_ref, wd_ref):
    xn = xn_ref[...]
    gate = jnp.dot(xn, wg_ref[0], preferred_element_type=F32)
    up = jnp.dot(xn, wu_ref[0], preferred_element_type=F32)
    act = (gate * _sigmoid(gate) * up).astype(BF16)
    return jnp.dot(act, wd_ref[0], preferred_element_type=F32)


def _ffn_kernel(x_ref, g_ref, wg_ref, wu_ref, wd_ref, o_ref, xn_ref):
    j = pl.program_id(1)

    @pl.when(j == 0)
    def _():
        xn_ref[...] = _rms(x_ref[...], g_ref[...]).astype(BF16)

    part = _swiglu_partial(xn_ref, wg_ref, wu_ref, wd_ref)

    @pl.when(j == 0)
    def _():
        o_ref[...] = part + x_ref[...]

    @pl.when(j > 0)
    def _():
        o_ref[...] += part


def ffn_residual(x, g, wg, wu, wd):
    m, d = x.shape
    f = wg.shape[2]
    tm = _pick_tile(m, (ROW_TILE_BIG, TIME_TILE))
    tf = COL_TILE
    return pl.pallas_call(
        _ffn_kernel,
        out_shape=jax.ShapeDtypeStruct((m, d), F32),
        grid=(m // tm, f // tf),
        in_specs=[pl.BlockSpec((tm, d), lambda i, j: (i, 0)),
                  pl.BlockSpec((1, d), lambda i, j: (0, 0)),
                  pl.BlockSpec((1, d, tf), lambda i, j: (0, 0, j)),
                  pl.BlockSpec((1, d, tf), lambda i, j: (0, 0, j)),
                  pl.BlockSpec((1, tf, d), lambda i, j: (0, j, 0))],
        out_specs=pl.BlockSpec((tm, d), lambda i, j: (i, 0)),
        scratch_shapes=[pltpu.VMEM((tm, d), BF16)],
        compiler_params=_cparams(("parallel", "arbitrary")),
        name="ffn_residual",
    )(x, g, wg, wu, wd)


def _experts_kernel(eid_ref, nused_ref, x_ref, g_ref, wg_ref, wu_ref, wd_ref, o_ref,
                    xn_ref, acc_ref):
    i = pl.program_id(0)
    j = pl.program_id(1)
    tm, d = xn_ref.shape
    nchunk = d // LANES

    @pl.when(i < nused_ref[0])
    def _():
        @pl.when(j == 0)
        def _():
            ssq = jnp.zeros((tm, LANES), F32)
            for c in range(nchunk):
                xc = _chunk_rows(x_ref, c, tm, nchunk)
                ssq = ssq + xc * xc
            inv = lax.rsqrt(jnp.sum(ssq, axis=1, keepdims=True) / d + EPS)
            for c in range(nchunk):
                sl = slice(c * LANES, (c + 1) * LANES)
                xn_ref[:, sl] = (_chunk_rows(x_ref, c, tm, nchunk) * inv * g_ref[:, sl]).astype(BF16)

        part = _swiglu_partial(xn_ref, wg_ref, wu_ref, wd_ref)

        @pl.when(j == 0)
        def _():
            acc_ref[...] = part

        @pl.when(j > 0)
        def _():
            acc_ref[...] += part

    @pl.when(j == pl.num_programs(1) - 1)
    def _():
        @pl.when(i >= nused_ref[0])
        def _():
            acc_ref[...] = jnp.zeros_like(acc_ref)

        for c in range(nchunk):
            o_ref[pl.ds(c, tm, stride=nchunk), :] = acc_ref[:, c * LANES:(c + 1) * LANES]


def ffn_experts(x, g, wg, wu, wd, eid, nused, tm):
    d = wg.shape[1]
    f = wg.shape[2]
    nchunk = d // LANES
    nblk = x.shape[0] // (tm * nchunk)
    tf = COL_TILE
    nj = f // tf

    def jj(i, j, nused):
        return jnp.where(i < nused[0], j, nj - 1)

    grid_spec = pltpu.PrefetchScalarGridSpec(
        num_scalar_prefetch=2,
        grid=(nblk, nj),
        in_specs=[pl.BlockSpec((tm * nchunk, LANES), lambda i, j, eid, nu: (i, 0)),
                  pl.BlockSpec((1, d), lambda i, j, eid, nu: (0, 0)),
                  pl.BlockSpec((1, d, tf), lambda i, j, eid, nu: (eid[i], 0, jj(i, j, nu))),
                  pl.BlockSpec((1, d, tf), lambda i, j, eid, nu: (eid[i], 0, jj(i, j, nu))),
                  pl.BlockSpec((1, tf, d), lambda i, j, eid, nu: (eid[i], jj(i, j, nu), 0))],
        out_specs=pl.BlockSpec((tm * nchunk, LANES), lambda i, j, eid, nu: (i, 0)),
        scratch_shapes=[pltpu.VMEM((tm, d), BF16), pltpu.VMEM((tm, d), F32)],
    )
    return pl.pallas_call(
        _experts_kernel,
        out_shape=jax.ShapeDtypeStruct(x.shape, F32),
        grid_spec=grid_spec,
        compiler_params=_cparams(("parallel", "arbitrary")),
        name="ffn_experts",
    )(eid, nused, x, g, wg, wu, wd)


def _lru_kernel(xb_ref, yb_ref, cw_ref, cb_ref, wr_ref, br_ref, wi_ref, bi_ref, lam_ref,
                o_ref, xext_ref, h_ref):
    tb, w = xb_ref.shape

    @pl.when(pl.program_id(1) == 0)
    def _():
        xext_ref[0:SUBLANES, :] = jnp.zeros((SUBLANES, w), F32)
        h_ref[...] = jnp.zeros_like(h_ref)

    xext_ref[SUBLANES:SUBLANES + tb, :] = xb_ref[...].astype(F32)
    cw = cw_ref[...]
    xc = cb_ref[...] + cw[CONV_W - 1:CONV_W, :] * xext_ref[SUBLANES:SUBLANES + tb, :]
    for j in range(CONV_W - 1):
        xc = xc + cw[j:j + 1, :] * xext_ref[pl.ds(SUBLANES - (CONV_W - 1) + j, tb), :]
    xext_ref[0:SUBLANES, :] = xext_ref[tb:tb + SUBLANES, :]

    xcb = xc.astype(BF16)
    nb = wr_ref.shape[0]
    bw = w // nb
    r_parts, i_parts = [], []
    for n in range(nb):
        xs = xcb[:, n * bw:(n + 1) * bw]
        r_parts.append(jnp.dot(xs, wr_ref[n], preferred_element_type=F32))
        i_parts.append(jnp.dot(xs, wi_ref[n], preferred_element_type=F32))
    r = _sigmoid(jnp.concatenate(r_parts, axis=1) + br_ref[...])
    ig = _sigmoid(jnp.concatenate(i_parts, axis=1) + bi_ref[...])

    neg_lam = -lam_ref[...]
    softplus = jnp.maximum(neg_lam, 0.0) + jnp.log1p(jnp.exp(-jnp.abs(neg_lam)))
    log_a = -LRU_C * r * softplus
    a = jnp.exp(log_a)
    u = jnp.sqrt(1.0 - a * a) * (ig * xc)

    row = lax.broadcasted_iota(I32, (tb, w), 0)
    s = 1
    while s < tb:
        keep = row >= s
        a_sh = pltpu.roll(a, s, 0)
        u_sh = pltpu.roll(u, s, 0)
        u = jnp.where(keep, a * u_sh + u, u)
        a = jnp.where(keep, a * a_sh, a)
        s *= 2
    h = a * h_ref[0:1, :] + u
    h_ref[0:1, :] = h[tb - 1:tb, :]

    y = yb_ref[...].astype(F32)
    gelu = 0.5 * y * (1.0 + jnp.tanh(0.7978845608028654 * (y + 0.044715 * (y * y * y))))
    o_ref[...] = (h * gelu).astype(o_ref.dtype)


def rg_lru(z, conv_w, conv_b, w_r, b_r, w_i, b_i, lam):
    b, tp, _ = z.shape
    w = conv_w.shape[1]
    tb = TIME_TILE
    nb = w_r.shape[0]
    bw = w // nb
    full = lambda shape: pl.BlockSpec(shape, lambda bi, ti: (0,) * len(shape))
    return pl.pallas_call(
        _lru_kernel,
        out_shape=jax.ShapeDtypeStruct((b, tp, w), BF16),
        grid=(b, tp // tb),
        in_specs=[pl.BlockSpec((None, tb, w), lambda bi, ti: (bi, ti, 0)),
                  pl.BlockSpec((None, tb, w), lambda bi, ti: (bi, ti, 1)),
                  full((CONV_W, w)), full((1, w)),
                  full((nb, bw, bw)), full((1, w)),
                  full((nb, bw, bw)), full((1, w)),
                  full((1, w))],
        out_specs=pl.BlockSpec((None, tb, w), lambda bi, ti: (bi, ti, 0)),
        scratch_shapes=[pltpu.VMEM((tb + 2 * SUBLANES, w), F32), pltpu.VMEM((SUBLANES, w), F32)],
        compiler_params=_cparams(("parallel", "arbitrary")),
        name="rg_lru",
    )(z, z, conv_w, conv_b, w_r, b_r, w_i, b_i, lam)


def _gla_kernel(q_ref, k_ref, v_ref, g_ref, glr_ref, wgk_ref, bgk_ref, ng_ref, o_ref, st_ref,
                *, chunk, heads):
    tb, hdk = q_ref.shape
    dk = hdk // heads
    dv = v_ref.shape[1] // heads
    c = chunk

    @pl.when(pl.program_id(1) == 0)
    def _():
        st_ref[...] = jnp.zeros_like(st_ref)

    row = lax.broadcasted_iota(I32, (c, hdk), 0)
    tri = lax.broadcasted_iota(I32, (c, c), 0) >= lax.broadcasted_iota(I32, (c, c), 1)
    nt = (((1,), (1,)), ((), ()))
    tn = (((0,), (0,)), ((), ()))

    def body(ci, carry):
        sl = pl.ds(pl.multiple_of(ci * c, c), c)
        pre = jnp.dot(glr_ref[sl, :], wgk_ref[...], preferred_element_type=F32) + bgk_ref[...]
        gk = (jnp.minimum(pre, 0.0) - jnp.log1p(jnp.exp(-jnp.abs(pre)))) * (1.0 / GLA_GATE_NORM)
        bc = gk
        s = 1
        while s < c:
            bc = bc + jnp.where(row >= s, pltpu.roll(bc, s, 0), 0.0)
            s *= 2
        q = q_ref[sl, :].astype(F32) * (dk ** -0.5)
        k = k_ref[sl, :].astype(F32)
        v = v_ref[sl, :]
        g = g_ref[sl, :].astype(F32)
        bmid = bc[c // 2:c // 2 + 1, :]
        blast = bc[c - 1:c, :]
        q_inter = (q * jnp.exp(bc)).astype(BF16)
        q_intra = (q * jnp.exp(bc - bmid)).astype(BF16)
        k_intra = (k * jnp.exp(bmid - bc)).astype(BF16)
        k_state = (k * jnp.exp(blast - bc)).astype(BF16)
        decay = jnp.exp(blast)
        outs = []
        for h in range(heads):
            ks = slice(h * dk, (h + 1) * dk)
            vs = slice(h * dv, (h + 1) * dv)
            st = st_ref[h]
            vh = v[:, vs]
            inter = lax.dot_general(q_inter[:, ks], st.astype(BF16), nt, preferred_element_type=F32)
            att = lax.dot_general(q_intra[:, ks], k_intra[:, ks], nt, preferred_element_type=F32)
            att = jnp.where(tri, att, 0.0).astype(BF16)
            o = inter + jnp.dot(att, vh, preferred_element_type=F32)
            st_ref[h] = st * decay[:, ks] + lax.dot_general(vh, k_state[:, ks], tn,
                                                           preferred_element_type=F32)
            o = _rms(o, ng_ref[...])
            gh = g[:, vs]
            outs.append(o * (gh * _sigmoid(gh)))
        o_ref[sl, :] = jnp.concatenate(outs, axis=1).astype(o_ref.dtype)
        return carry

    lax.fori_loop(0, tb // c, body, 0)


def gla(z, col0, w_gk, b_gk, norm_g, hdk, hdv):
    b, tp, _ = z.shape
    tb = TIME_TILE
    cq, ck = col0 // hdk, col0 // hdk + 1
    cv = (col0 + 2 * hdk) // hdv
    cg = cv + 1
    cl = (col0 + 2 * hdk + 2 * hdv) // LANES
    full = lambda shape: pl.BlockSpec(shape, lambda bi, ti: (0,) * len(shape))
    return pl.pallas_call(
        functools.partial(_gla_kernel, chunk=GLA_CHUNK, heads=GLA_HEADS),
        out_shape=jax.ShapeDtypeStruct((b, tp, hdv), BF16),
        grid=(b, tp // tb),
        in_specs=[pl.BlockSpec((None, tb, hdk), lambda bi, ti: (bi, ti, cq)),
                  pl.BlockSpec((None, tb, hdk), lambda bi, ti: (bi, ti, ck)),
                  pl.BlockSpec((None, tb, hdv), lambda bi, ti: (bi, ti, cv)),
                  pl.BlockSpec((None, tb, hdv), lambda bi, ti: (bi, ti, cg)),
                  pl.BlockSpec((None, tb, LANES), lambda bi, ti: (bi, ti, cl)),
                  full((LANES, hdk)), full((1, hdk)), full((1, hdv // GLA_HEADS))],
        out_specs=pl.BlockSpec((None, tb, hdv), lambda bi, ti: (bi, ti, 0)),
        scratch_shapes=[pltpu.VMEM((GLA_HEADS, hdv // GLA_HEADS, hdk // GLA_HEADS), F32)],
        compiler_params=_cparams(("parallel", "arbitrary")),
        name="gla",
    )(z, z, z, z, z, w_gk, b_gk, norm_g)


def _rope_tile(x, cos, sin_up, sin_dn, half):
    return x * cos + pltpu.roll(x, half, 1) * sin_up + pltpu.roll(x, LANES - half, 1) * sin_dn


def _rope_kernel(qk_ref, qi_ref, kw_ref, tab_ref, qk_out, qi_out, ki_out, wi_out):
    def tabs(kind):
        return tab_ref[3 * kind], tab_ref[3 * kind + 1], tab_ref[3 * kind + 2]

    n_q_tiles = ATT_HEADS
    for t in range(qk_ref.shape[1] // LANES):
        ca, sa_up, sa_dn = tabs(0 if t < n_q_tiles else 1)
        sl = slice(t * LANES, (t + 1) * LANES)
        qk_out[:, sl] = _rope_tile(qk_ref[:, sl].astype(F32), ca, sa_up, sa_dn, 16).astype(BF16)
    cb, sb_up, sb_dn = tabs(2)
    for t in range(qi_ref.shape[1] // LANES):
        sl = slice(t * LANES, (t + 1) * LANES)
        qi_out[:, sl] = _rope_tile(qi_ref[:, sl].astype(F32), cb, sb_up, sb_dn, 8).astype(BF16)
    ck, sk_up, sk_dn = tabs(3)
    kw = _rope_tile(kw_ref[...].astype(F32), ck, sk_up, sk_dn, 8)
    ki_out[...] = kw.astype(BF16)
    wi_out[...] = pltpu.roll(kw, LANES - IDX_DIM, 1)


def rope_tables(tp, head_dim):
    pos = jnp.arange(tp, dtype=F32)[:, None]
    lane = jnp.arange(LANES)

    def tables(dim, period, active):
        half = (dim // 4) // 2
        freqs = ROPE_THETA ** (-jnp.arange(half, dtype=F32) / half)
        ang = pos * freqs[None, :]
        cos, sin = jnp.cos(ang), jnp.sin(ang)
        within = lane % period
        fidx = within % half
        cos_l, sin_l = cos[:, fidx], sin[:, fidx]
        lo = (within < half) & active
        hi = (within >= half) & (within < 2 * half) & active
        c = jnp.where((lo | hi)[None, :], cos_l, 1.0)
        s_up = jnp.where(hi[None, :], sin_l, 0.0)
        s_dn = jnp.where(lo[None, :], -sin_l, 0.0)
        return c, s_up, s_dn

    all_on = jnp.ones((LANES,), bool)
    a = tables(head_dim, head_dim, all_on)
    bq = tables(IDX_DIM, IDX_DIM, all_on)
    ck, sk_up, sk_dn = tables(IDX_DIM, IDX_DIM, lane < IDX_DIM)
    wscale = IDX_HEADS ** -0.5 * IDX_DIM ** -0.5
    ck = jnp.where((lane >= IDX_DIM)[None, :],
                   jnp.where(lane < IDX_DIM + IDX_HEADS, wscale, 0.0)[None, :], ck)
    a_scaled = [t * (head_dim ** -0.5 * LOG2_E) for t in a]
    return jnp.stack([*a_scaled, *a, *bq, ck, sk_up, sk_dn])


def rope(z, tabs, tp, nqk, nqi, col_qi, col_kw):
    m = z.shape[0]
    tm = TIME_TILE
    nper = tp // tm
    return pl.pallas_call(
        _rope_kernel,
        out_shape=[jax.ShapeDtypeStruct((m, nqk), BF16), jax.ShapeDtypeStruct((m, nqi), BF16),
                   jax.ShapeDtypeStruct((m, LANES), BF16), jax.ShapeDtypeStruct((m, LANES), F32)],
        grid=(m // tm,),
        in_specs=[pl.BlockSpec((tm, nqk), lambda i: (i, 0)),
                  pl.BlockSpec((tm, nqi), lambda i: (i, col_qi // nqi)),
                  pl.BlockSpec((tm, LANES), lambda i: (i, col_kw // LANES)),
                  pl.BlockSpec((12, tm, LANES), lambda i: (0, i % nper, 0))],
        out_specs=[pl.BlockSpec((tm, nqk), lambda i: (i, 0)),
                   pl.BlockSpec((tm, nqi), lambda i: (i, 0)),
                   pl.BlockSpec((tm, LANES), lambda i: (i, 0)),
                   pl.BlockSpec((tm, LANES), lambda i: (i, 0))],
        compiler_params=_cparams(("parallel",)),
        name="rope",
    )(z, z, z, tabs)


def _index_keys(qi_stack, w_cols, ki_blk, q0, k0):
    nt = (((1,), (1,)), ((), ()))
    tq = w_cols[0].shape[0]
    lg = lax.dot_general(qi_stack, ki_blk, nt, preferred_element_type=F32)
    score = None
    for h, wh in enumerate(w_cols):
        term = wh * jnp.maximum(lg[h * tq:(h + 1) * tq, :], 0.0)
        score = term if score is None else score + term
    tk = score.shape[1]
    qpos = q0 + lax.broadcasted_iota(I32, (tq, tk), 0)
    kpos = k0 + lax.broadcasted_iota(I32, (tq, tk), 1)
    score = jnp.where(kpos <= qpos, jnp.where(kpos < N_META_TOK, BIG, score), NEG)
    bits = pltpu.bitcast(score, I32)
    return jnp.where(bits < 0, bits ^ jnp.int32(0x7FFFFFFF), bits)


def _float_key(x):
    bits = int(np.float32(x).view(np.int32))
    return bits ^ 0x7FFFFFFF if bits < 0 else bits


def _indexer_kernel(qi_ref, wi_ref, ki_ref, bias_ref, keys_ref, qs_ref, cut_ref, *, k_sel, tk):
    tq = qi_ref.shape[0]
    ntile = tk // LANES
    qb = pl.program_id(1)
    q0 = qb * tq
    nkv = (q0 + tq + tk - 1) // tk
    for h in range(IDX_HEADS):
        qs_ref[h * tq:(h + 1) * tq, :] = qi_ref[:, h * IDX_DIM:(h + 1) * IDX_DIM]
    w_cols = [wi_ref[:, h:h + 1] for h in range(IDX_HEADS)]

    def fill(kb, carry):
        k0 = pl.multiple_of(kb * tk, tk)
        keys_ref[:, pl.ds(k0, tk)] = _index_keys(qs_ref[...], w_cols,
                                                 ki_ref[pl.ds(k0, tk), 0:IDX_DIM], q0, k0)
        return carry

    lax.fori_loop(0, nkv, fill, 0)

    lane = lax.broadcasted_iota(I32, (tq, LANES), 1)

    def tiles(kb):
        k0 = pl.multiple_of(kb * tk, tk)
        keys = keys_ref[:, pl.ds(k0, tk)]
        return [(keys[:, t * LANES:(t + 1) * LANES], k0 + t * LANES + lane) for t in range(ntile)]

    def count(pred):
        def blk(kb, acc):
            for kt, pt in tiles(kb):
                acc = acc + jnp.where(pred(kt, pt), 1, 0)
            return acc
        acc = lax.fori_loop(0, nkv, blk, jnp.zeros((tq, LANES), I32))
        return jnp.broadcast_to(jnp.sum(acc, axis=1, keepdims=True), (tq, LANES))

    def bit_step(bi, thr):
        cand = thr + jnp.left_shift(jnp.int32(1), 31 - bi)
        n = count(lambda keys, kpos: keys >= cand)
        return jnp.where(n >= k_sel, cand, thr)

    thr = lax.fori_loop(0, 32, bit_step, jnp.full((tq, LANES), INT_MIN, I32))
    n_ge = count(lambda keys, kpos: keys >= thr)
    no_tie_cut = jnp.full((tq, LANES), 1 << 30, I32)
    cut_ref[...] = no_tie_cut

    @pl.when(jnp.max(n_ge) > k_sel)
    def _():
        n_gt = count(lambda keys, kpos: keys > thr)
        need = k_sel - n_gt

        def cut_step(bi, cut):
            cand = cut + jnp.left_shift(jnp.int32(1), 15 - bi)
            n = count(lambda keys, kpos: (keys == thr) & (kpos < cand))
            return jnp.where(n <= need, cand, cut)

        cut = lax.fori_loop(0, 16, cut_step, jnp.zeros((tq, LANES), I32))
        cut_ref[...] = jnp.where(n_ge > k_sel, cut, no_tie_cut)

    bias_ref[...] = jnp.full(bias_ref.shape, NEG, BF16)
    cut = cut_ref[...]
    key_valid = jnp.int32(_float_key(0.5 * NEG))

    def emit(kb, carry):
        k0 = pl.multiple_of(kb * tk, tk)
        out = []
        for kt, pt in tiles(kb):
            chosen = ((kt > thr) | ((kt == thr) & (pt < cut))) & (kt > key_valid)
            out.append(jnp.where(chosen, 0.0, NEG).astype(BF16))
        bias_ref[:, pl.ds(k0, tk)] = jnp.concatenate(out, axis=1)
        return carry

    lax.fori_loop(0, nkv, emit, 0)


def indexer(qi, wi, ki, b, tp, k_sel):
    tq = INDEX_Q_TILE
    nq = tp // tq
    return pl.pallas_call(
        functools.partial(_indexer_kernel, k_sel=k_sel, tk=TIME_TILE),
        out_shape=jax.ShapeDtypeStruct((b * tp, tp), BF16),
        grid=(b, nq),
        in_specs=[pl.BlockSpec((tq, qi.shape[1]), lambda bi, qb: (bi * nq + qb, 0)),
                  pl.BlockSpec((tq, LANES), lambda bi, qb: (bi * nq + qb, 0)),
                  pl.BlockSpec((tp, LANES), lambda bi, qb: (bi, 0))],
        out_specs=pl.BlockSpec((tq, tp), lambda bi, qb: (bi * nq + qb, 0)),
        scratch_shapes=[pltpu.VMEM((tq, tp), I32), pltpu.VMEM((IDX_HEADS * tq, IDX_DIM), BF16),
                        pltpu.VMEM((tq, LANES), I32)],
        compiler_params=_cparams(("parallel", "arbitrary")),
        name="indexer",
    )(qi, wi, ki)


def _attn_kernel(q_ref, k_ref, v_ref, bias_ref, o_ref, qs_ref, m_ref, l_ref, acc_ref):
    tq = q_ref.shape[0]
    tk = k_ref.shape[0]
    hd = k_ref.shape[1] // ATT_KV_HEADS
    grows = ATT_GROUPS * tq
    qb = pl.program_id(1)
    kb = pl.program_id(2)
    nt = (((1,), (1,)), ((), ()))

    @pl.when(kb == 0)
    def _():
        m_ref[...] = jnp.full_like(m_ref, NEG)
        l_ref[...] = jnp.zeros_like(l_ref)
        acc_ref[...] = jnp.zeros_like(acc_ref)
        for hi in range(ATT_HEADS):
            qs_ref[hi * tq:(hi + 1) * tq, :] = q_ref[:, hi * hd:(hi + 1) * hd]

    @pl.when(kb <= qb)
    def _():
        bias = bias_ref[...].astype(F32)[None]
        ones = jnp.ones((tk, LANES), BF16)
        for kvh in range(ATT_KV_HEADS):
            rows = slice(kvh * grows, (kvh + 1) * grows)
            kh = k_ref[:, kvh * hd:(kvh + 1) * hd]
            vh1 = jnp.concatenate([v_ref[:, kvh * hd:(kvh + 1) * hd], ones], axis=1)
            s = lax.dot_general(qs_ref[rows, :], kh, nt, preferred_element_type=F32)
            s = (s.reshape(ATT_GROUPS, tq, tk) + bias).reshape(grows, tk)
            m_old = m_ref[rows, :]
            m_new = jnp.maximum(m_old, jnp.broadcast_to(jnp.max(s, axis=1, keepdims=True),
                                                        (grows, LANES)))
            p = jnp.concatenate(
                [jnp.exp2(s[:, t * LANES:(t + 1) * LANES] - m_new).astype(BF16)
                 for t in range(tk // LANES)], axis=1)
            alpha = jnp.exp2(m_old - m_new)
            pv = jnp.dot(p, vh1, preferred_element_type=F32)
            l_ref[rows, :] = alpha * l_ref[rows, :] + pv[:, hd:]
            acc_ref[rows, :] = alpha * acc_ref[rows, :] + pv[:, :hd]
            m_ref[rows, :] = m_new

    @pl.when(kb == pl.num_programs(2) - 1)
    def _():
        for hi in range(ATT_HEADS):
            rows = slice(hi * tq, (hi + 1) * tq)
            o_ref[:, hi * hd:(hi + 1) * hd] = (acc_ref[rows, :] / l_ref[rows, :]).astype(o_ref.dtype)


def sparse_attention(qk, z, bias, b, tp, d, col_v):
    tq = tk = TIME_TILE
    nq = tp // tq
    hd = d // ATT_HEADS
    kvw = ATT_KV_HEADS * hd
    kvrow = lambda bi, qb, kb: bi * nq + jnp.minimum(kb, qb)
    qrow = lambda bi, qb, kb: bi * nq + qb
    return pl.pallas_call(
        _attn_kernel,
        out_shape=jax.ShapeDtypeStruct((b * tp, d), BF16),
        grid=(b, nq, nq),
        in_specs=[pl.BlockSpec((tq, d), lambda bi, qb, kb: (qrow(bi, qb, kb), 0)),
                  pl.BlockSpec((tk, kvw), lambda bi, qb, kb: (kvrow(bi, qb, kb), d // kvw)),
                  pl.BlockSpec((tk, kvw), lambda bi, qb, kb: (kvrow(bi, qb, kb), col_v // kvw)),
                  pl.BlockSpec((tq, tk), lambda bi, qb, kb: (qrow(bi, qb, kb), jnp.minimum(kb, qb)))],
        out_specs=pl.BlockSpec((tq, d), lambda bi, qb, kb: (qrow(bi, qb, kb), 0)),
        scratch_shapes=[pltpu.VMEM((ATT_HEADS * tq, hd), BF16),
                        pltpu.VMEM((ATT_HEADS * tq, LANES), F32),
                        pltpu.VMEM((ATT_HEADS * tq, LANES), F32),
                        pltpu.VMEM((ATT_HEADS * tq, hd), F32)],
        compiler_params=_cparams(("parallel", "parallel", "arbitrary")),
        name="sparse_attention",
    )(qk, qk, z, bias)


def _router_kernel(x_ref, g_ref, w_ref, o_ref, xt_ref):
    tm, d = x_ref.shape
    nchunk = d // LANES
    for c in range(nchunk):
        xt_ref[pl.ds(c, tm, stride=nchunk), :] = x_ref[:, c * LANES:(c + 1) * LANES]
    xn = _rms(x_ref[...], g_ref[...]).astype(BF16)
    logits = jnp.dot(xn, w_ref[...], preferred_element_type=F32)
    lane = lax.broadcasted_iota(I32, logits.shape, 1)
    logits = jnp.where(lane < N_EXPERTS, logits, -jnp.inf)
    v1 = jnp.max(logits, axis=1, keepdims=True)
    e1 = jnp.min(jnp.where(logits == v1, lane, LANES), axis=1, keepdims=True)
    rest = jnp.where(lane == e1, -jnp.inf, logits)
    v2 = jnp.max(rest, axis=1, keepdims=True)
    e2 = jnp.min(jnp.where(rest == v2, lane, LANES), axis=1, keepdims=True)
    ex = jnp.exp(v2 - v1)
    g1 = 1.0 / (1.0 + ex)
    g2 = ex / (1.0 + ex)
    out = jnp.where(lane == 0, g1, jnp.where(lane == 1, g2, 0.0))
    out = jnp.where(lane == 2, e1.astype(F32), jnp.where(lane == 3, e2.astype(F32), out))
    o_ref[...] = out


def router(x, g, w):
    m, d = x.shape
    tm = TIME_TILE
    nchunk = d // LANES
    return pl.pallas_call(
        _router_kernel,
        out_shape=[jax.ShapeDtypeStruct((m, LANES), F32),
                   jax.ShapeDtypeStruct((m * nchunk, LANES), F32)],
        grid=(m // tm,),
        in_specs=[pl.BlockSpec((tm, d), lambda i: (i, 0)),
                  pl.BlockSpec((1, d), lambda i: (0, 0)),
                  pl.BlockSpec((d, LANES), lambda i: (0, 0))],
        out_specs=[pl.BlockSpec((tm, LANES), lambda i: (i, 0)),
                   pl.BlockSpec((tm * nchunk, LANES), lambda i: (i, 0))],
        compiler_params=_cparams(("parallel",)),
        name="router",
    )(x, g, w)


def _gather_kernel(idx_ref, src_ref, dst_ref, sem, *, rows, nchunk):
    base = pl.program_id(0) * rows

    def token_copy(r):
        src_row = pl.multiple_of(idx_ref[base + r] * nchunk, nchunk)
        dst_row = pl.multiple_of(r * nchunk, nchunk)
        return pltpu.make_async_copy(src_ref.at[pl.ds(src_row, nchunk)],
                                     dst_ref.at[pl.ds(dst_row, nchunk)], sem)

    def start(r, carry):
        token_copy(r).start()
        return carry

    def wait(r, carry):
        token_copy(r).wait()
        return carry

    lax.fori_loop(0, rows, start, 0)
    lax.fori_loop(0, rows, wait, 0)


def gather_tokens(src, idx, nchunk):
    n = idx.shape[0]
    rows = _pick_tile(n, (GATHER_ROWS, TIME_TILE, 256, 128))
    grid_spec = pltpu.PrefetchScalarGridSpec(
        num_scalar_prefetch=1,
        grid=(n // rows,),
        in_specs=[pl.BlockSpec(memory_space=pl.ANY)],
        out_specs=pl.BlockSpec((rows * nchunk, LANES), lambda i, idx: (i, 0)),
        scratch_shapes=[pltpu.SemaphoreType.DMA(())],
    )
    return pl.pallas_call(
        functools.partial(_gather_kernel, rows=rows, nchunk=nchunk),
        out_shape=jax.ShapeDtypeStruct((n * nchunk, LANES), src.dtype),
        grid_spec=grid_spec,
        compiler_params=_cparams(("arbitrary",)),
        name="gather_tokens",
    )(idx, src)


def _combine_norm_kernel(h_ref, y1_ref, y2_ref, route_ref, g_ref, o_ref):
    tm, d = h_ref.shape
    nchunk = d // LANES
    g1 = route_ref[:, 0:1]
    g2 = route_ref[:, 1:2]
    ssq = jnp.zeros((tm, LANES), F32)
    for c in range(nchunk):
        sl = slice(c * LANES, (c + 1) * LANES)
        hc = (h_ref[:, sl] + g1 * _chunk_rows(y1_ref, c, tm, nchunk)
              + g2 * _chunk_rows(y2_ref, c, tm, nchunk))
        o_ref[:, sl] = hc
        ssq = ssq + hc * hc
    inv = lax.rsqrt(jnp.sum(ssq, axis=1, keepdims=True) / d + EPS)
    o_ref[...] = o_ref[...] * inv * g_ref[...]


def combine_norm(h, y, route, g):
    m, d = h.shape
    tm = TIME_TILE
    nblk = m // tm
    nchunk = d // LANES
    return pl.pallas_call(
        _combine_norm_kernel,
        out_shape=jax.ShapeDtypeStruct((m, d), F32),
        grid=(nblk,),
        in_specs=[pl.BlockSpec((tm, d), lambda i: (i, 0)),
                  pl.BlockSpec((tm * nchunk, LANES), lambda i: (i, 0)),
                  pl.BlockSpec((tm * nchunk, LANES), lambda i: (i + nblk, 0)),
                  pl.BlockSpec((tm, LANES), lambda i: (i, 0)),
                  pl.BlockSpec((1, d), lambda i: (0, 0))],
        out_specs=pl.BlockSpec((tm, d), lambda i: (i, 0)),
        compiler_params=_cparams(("parallel",)),
        name="combine_norm",
    )(h, y, y, route, g)


def _pad_cols(w, n):
    return jnp.pad(w, ((0, 0), (0, n - w.shape[1])))


def _round_up(n, k):
    return -(-n // k) * k


def layer_ab(h, b, tp, norm_g, w_in, conv_w, conv_b, w_r, b_r, w_i, b_i, lam, w_gk, b_gk,
             gla_norm, w_out, ffn_norm, ffn_wg, ffn_wu, ffn_wd):
    m, d = h.shape
    lw = conv_w.shape[1]
    hdk = w_gk.shape[1]
    hdv = d - lw
    n_in = _round_up(w_in.shape[1] - GLA_RANK + LANES, COL_TILE)
    z = norm_matmul(h, norm_g[None], _pad_cols(w_in, n_in).astype(BF16)).reshape(b, tp, n_in)
    a_out = rg_lru(z, conv_w, conv_b[None], w_r.astype(BF16), b_r.reshape(1, lw),
                   w_i.astype(BF16), b_i.reshape(1, lw), lam[None])
    w_gk_p = jnp.pad(w_gk, ((0, LANES - GLA_RANK), (0, 0))).astype(BF16)
    b_out = gla(z, 2 * lw, w_gk_p, b_gk[None], gla_norm[None], hdk, hdv)
    h = matmul_residual([a_out.reshape(m, lw), b_out.reshape(m, hdv)], w_out.astype(BF16), h)
    return ffn_residual(h, ffn_norm[None], ffn_wg.astype(BF16)[None], ffn_wu.astype(BF16)[None],
                        ffn_wd.astype(BF16)[None])


def layer_c(h, b, tp, t_real, norm_g, w_in, w_out):
    m, d = h.shape
    hd = d // ATT_HEADS
    kvw = ATT_KV_HEADS * hd
    nqi = IDX_HEADS * IDX_DIM
    col_v = d + kvw
    col_qi = d + 2 * kvw
    col_kw = col_qi + nqi
    n_in = _round_up(col_kw + LANES, COL_TILE)
    z = norm_matmul(h, norm_g[None], _pad_cols(w_in, n_in).astype(BF16))
    qk, qi, ki, wi = rope(z, rope_tables(tp, hd), tp, d + kvw, nqi, col_qi, col_kw)
    k_sel = min(TOPK_MAX, (t_real - N_META_TOK) // 4)
    bias = indexer(qi, wi, ki, b, tp, k_sel)
    att = sparse_attention(qk, z, bias, b, tp, d, col_v)
    return matmul_residual([att], w_out.astype(BF16), h)


def moe_dispatch(route, tm):
    m = route.shape[0]
    e_flat = route[:, 2:4].astype(I32).reshape(-1)
    onehot = (e_flat[:, None] == jnp.arange(N_EXPERTS)[None, :]).astype(I32)
    rank = jnp.take_along_axis(jnp.cumsum(onehot, axis=0) - onehot, e_flat[:, None], axis=1)[:, 0]
    counts = jnp.sum(onehot, axis=0)
    padded = (counts + tm - 1) // tm * tm
    pad_end = jnp.cumsum(padded)
    dest = (pad_end - padded)[e_flat] + rank
    nblk = -(-(2 * m) // tm) + N_EXPERTS
    slot_tok = jnp.zeros((nblk * tm,), I32).at[dest].set(jnp.arange(2 * m, dtype=I32) // 2)
    nused = (pad_end[-1] // tm).astype(I32)
    blk = jnp.minimum(jnp.arange(nblk, dtype=I32), nused - 1)
    blk_expert = jnp.minimum(jnp.sum((pad_end[None, :] <= (blk * tm)[:, None]).astype(I32), axis=1),
                             N_EXPERTS - 1)
    return slot_tok, dest.reshape(m, 2), blk_expert, nused.reshape(1)


def layer_moe_final(h, norm_g, w_router, wg, wu, wd, final_g):
    m, d = h.shape
    nchunk = d // LANES
    route, h_tok = router(h, norm_g[None], _pad_cols(w_router, LANES).astype(BF16))
    tm = _pick_tile(m, (ROW_TILE_BIG, TIME_TILE))
    slot_tok, tok_slots, blk_expert, nused = moe_dispatch(route, tm)
    x_slots = gather_tokens(h_tok, slot_tok, nchunk)
    y_slots = ffn_experts(x_slots, norm_g[None], wg.astype(BF16), wu.astype(BF16), wd.astype(BF16),
                          blk_expert, nused, tm)
    y_tok = gather_tokens(y_slots, tok_slots.T.reshape(-1), nchunk)
    return combine_norm(h, y_tok, route, final_g[None])


def kernel(x, meta, ab_norm, ab_w_in, lru_conv_w, lru_conv_b, lru_w_r, lru_b_r, lru_w_i, lru_b_i,
           lru_lam, gla_w_gk, gla_b_gk, gla_norm, ab_w_out, ffn_norm, ffn_w_gate, ffn_w_up,
           ffn_w_down, c_norm, c_w_in, c_w_out, moe_norm, moe_router, moe_w_gate, moe_w_up,
           moe_w_down, final_norm):
    b, seq, d = x.shape
    assert ab_norm.shape[0] == 1 and c_norm.shape[0] == 1, "two-layer trunk only"
    t_real = seq + N_META_TOK
    tp = _round_up(t_real, TIME_TILE)
    h = jnp.concatenate([jnp.broadcast_to(meta.astype(x.dtype)[None], (b, N_META_TOK, d)), x,
                         jnp.zeros((b, tp - t_real, d), x.dtype)], axis=1).reshape(b * tp, d)
    h = layer_ab(h, b, tp, ab_norm[0], ab_w_in[0], lru_conv_w[0], lru_conv_b[0], lru_w_r[0],
                 lru_b_r[0], lru_w_i[0], lru_b_i[0], lru_lam[0], gla_w_gk[0], gla_b_gk[0],
                 gla_norm[0], ab_w_out[0], ffn_norm[0], ffn_w_gate[0], ffn_w_up[0], ffn_w_down[0])
    h = layer_c(h, b, tp, t_real, c_norm[0], c_w_in[0], c_w_out[0])
    out = layer_moe_final(h, moe_norm[0], moe_router[0], moe_w_gate[0], moe_w_up[0],
                          moe_w_down[0], final_norm)
    return out.reshape(b, tp, d)[:, N_META_TOK:t_real]
```

```python
import functools

import jax
import jax.numpy as jnp
import numpy as np
from jax import lax
from jax.experimental import pallas as pl
from jax.experimental.pallas import tpu as pltpu

F32 = jnp.float32
BF16 = jnp.bfloat16
I32 = jnp.int32

N_META_TOK = 16
EPS = 1e-6
ROPE_THETA = 500000.0
LRU_BLOCKS = 4
CONV_W = 4
LRU_C = 8.0
GLA_HEADS = 4
GLA_RANK = 16
GLA_GATE_NORM = 16.0
ATT_HEADS = 16
ATT_KV_HEADS = 4
ATT_GROUPS = ATT_HEADS // ATT_KV_HEADS
IDX_HEADS = 8
IDX_DIM = 64
TOPK_MAX = 256
BIG = 1e30
NEG = -1e30
N_EXPERTS = 8
LOG2_E = 1.4426950408889634

LANES = 128
SUBLANES = 8
TIME_TILE = 384
INDEX_Q_TILE = 128
ROW_TILE_BIG = 768
COL_TILE = 512
GLA_CHUNK = 64
GATHER_ROWS = 512
VMEM_LIMIT = 56 * 1024 * 1024
INT_MIN = -(2 ** 31)


def _cparams(sem):
    return pltpu.CompilerParams(dimension_semantics=sem, vmem_limit_bytes=VMEM_LIMIT)


def _rms(x, g):
    return x * lax.rsqrt(jnp.mean(x * x, axis=-1, keepdims=True) + EPS) * g


def _sigmoid(x):
    return 1.0 / (1.0 + jnp.exp(-x))


def _pick_tile(n, candidates):
    for c in candidates:
        if n % c == 0:
            return c
    raise ValueError(f"no tile in {candidates} divides {n}")


def _norm_matmul_kernel(x_ref, g_ref, w_ref, o_ref, xn_ref):
    @pl.when(pl.program_id(1) == 0)
    def _():
        xn_ref[...] = _rms(x_ref[...], g_ref[...]).astype(BF16)

    o_ref[...] = jnp.dot(xn_ref[...], w_ref[...], preferred_element_type=F32).astype(o_ref.dtype)


def norm_matmul(x, g, w):
    m, d = x.shape
    n = w.shape[1]
    tm = _pick_tile(m, (ROW_TILE_BIG, TIME_TILE))
    tn = COL_TILE
    return pl.pallas_call(
        _norm_matmul_kernel,
        out_shape=jax.ShapeDtypeStruct((m, n), BF16),
        grid=(m // tm, n // tn),
        in_specs=[pl.BlockSpec((tm, d), lambda i, j: (i, 0)),
                  pl.BlockSpec((1, d), lambda i, j: (0, 0)),
                  pl.BlockSpec((d, tn), lambda i, j: (0, j))],
        out_specs=pl.BlockSpec((tm, tn), lambda i, j: (i, j)),
        scratch_shapes=[pltpu.VMEM((tm, d), BF16)],
        compiler_params=_cparams(("parallel", "arbitrary")),
        name="norm_matmul",
    )(x, g, w)


def _matmul_res_kernel(*refs, nparts):
    xs = refs[:nparts]
    w_ref, r_ref, o_ref = refs[nparts:]
    acc = r_ref[...]
    k0 = 0
    for x_ref in xs:
        kp = x_ref.shape[1]
        acc = acc + jnp.dot(x_ref[...], w_ref[k0:k0 + kp, :], preferred_element_type=F32)
        k0 += kp
    o_ref[...] = acc


def matmul_residual(parts, w, res):
    m, n = res.shape
    k = w.shape[0]
    tm = _pick_tile(m, (ROW_TILE_BIG, TIME_TILE))
    tn = 1024
    in_specs = [pl.BlockSpec((tm, p.shape[1]), lambda i, j: (i, 0)) for p in parts]
    in_specs += [pl.BlockSpec((k, tn), lambda i, j: (0, j)),
                 pl.BlockSpec((tm, tn), lambda i, j: (i, j))]
    return pl.pallas_call(
        functools.partial(_matmul_res_kernel, nparts=len(parts)),
        out_shape=jax.ShapeDtypeStruct((m, n), F32),
        grid=(m // tm, n // tn),
        in_specs=in_specs,
        out_specs=pl.BlockSpec((tm, tn), lambda i, j: (i, j)),
        compiler_params=_cparams(("parallel", "arbitrary")),
        name="matmul_residual",
    )(*parts, w, res)


def _chunk_rows(ref, c, rows, nchunk):
    return ref[pl.ds(c, rows, stride=nchunk), :]


def _swiglu_partial(xn_ref, wg_ref, wu_ref, wd_ref):
    xn = xn_ref[...]
    gate = jnp.dot(xn, wg_ref[0], preferred_element_type=F32)
    up = jnp.dot(xn, wu_ref[0], preferred_element_type=F32)
    act = (gate * _sigmoid(gate) * up).astype(BF16)
    return jnp.dot(act, wd_ref[0], preferred_element_type=F32)


def _ffn_kernel(x_ref, g_ref, wg_ref, wu_ref, wd_ref, o_ref, xn_ref):
    j = pl.program_id(1)

    @pl.when(j == 0)
    def _():
        xn_ref[...] = _rms(x_ref[...], g_ref[...]).astype(BF16)

    part = _swiglu_partial(xn_ref, wg_ref, wu_ref, wd_ref)

    @pl.when(j == 0)
    def _():
        o_ref[...] = part + x_ref[...]

    @pl.when(j > 0)
    def _():
        o_ref[...] += part


def ffn_residual(x, g, wg, wu, wd):
    m, d = x.shape
    f = wg.shape[2]
    tm = _pick_tile(m, (ROW_TILE_BIG, TIME_TILE))
    tf = COL_TILE
    return pl.pallas_call(
        _ffn_kernel,
        out_shape=jax.ShapeDtypeStruct((m, d), F32),
        grid=(m // tm, f // tf),
        in_specs=[pl.BlockSpec((tm, d), lambda i, j: (i, 0)),
                  pl.BlockSpec((1, d), lambda i, j: (0, 0)),
                  pl.BlockSpec((1, d, tf), lambda i, j: (0, 0, j)),
                  pl.BlockSpec((1, d, tf), lambda i, j: (0, 0, j)),
                  pl.BlockSpec((1, tf, d), lambda i, j: (0, j, 0))],
        out_specs=pl.BlockSpec((tm, d), lambda i, j: (i, 0)),
        scratch_shapes=[pltpu.VMEM((tm, d), BF16)],
        compiler_params=_cparams(("parallel", "arbitrary")),
        name="ffn_residual",
    )(x, g, wg, wu, wd)


def _experts_kernel(eid_ref, nused_ref, x_ref, g_ref, wg_ref, wu_ref, wd_ref, o_ref,
                    xn_ref, acc_ref):
    i = pl.program_id(0)
    j = pl.program_id(1)
    tm, d = xn_ref.shape
    nchunk = d // LANES

    @pl.when(i < nused_ref[0])
    def _():
        @pl.when(j == 0)
        def _():
            ssq = jnp.zeros((tm, LANES), F32)
            for c in range(nchunk):
                xc = _chunk_rows(x_ref, c, tm, nchunk)
                ssq = ssq + xc * xc
            inv = lax.rsqrt(jnp.sum(ssq, axis=1, keepdims=True) / d + EPS)
            for c in range(nchunk):
                sl = slice(c * LANES, (c + 1) * LANES)
                xn_ref[:, sl] = (_chunk_rows(x_ref, c, tm, nchunk) * inv * g_ref[:, sl]).astype(BF16)

        part = _swiglu_partial(xn_ref, wg_ref, wu_ref, wd_ref)

        @pl.when(j == 0)
        def _():
            acc_ref[...] = part

        @pl.when(j > 0)
        def _():
            acc_ref[...] += part

    @pl.when(j == pl.num_programs(1) - 1)
    def _():
        @pl.when(i >= nused_ref[0])
        def _():
            acc_ref[...] = jnp.zeros_like(acc_ref)

        for c in range(nchunk):
            o_ref[pl.ds(c, tm, stride=nchunk), :] = acc_ref[:, c * LANES:(c + 1) * LANES]


def ffn_experts(x, g, wg, wu, wd, eid, nused, tm):
    d = wg.shape[1]
    f = wg.shape[2]
    nchunk = d // LANES
    nblk = x.shape[0] // (tm * nchunk)
    tf = COL_TILE
    nj = f // tf

    def jj(i, j, nused):
        return jnp.where(i < nused[0], j, nj - 1)

    grid_spec = pltpu.PrefetchScalarGridSpec(
        num_scalar_prefetch=2,
        grid=(nblk, nj),
        in_specs=[pl.BlockSpec((tm * nchunk, LANES), lambda i, j, eid, nu: (i, 0)),
                  pl.BlockSpec((1, d), lambda i, j, eid, nu: (0, 0)),
                  pl.BlockSpec((1, d, tf), lambda i, j, eid, nu: (eid[i], 0, jj(i, j, nu))),
                  pl.BlockSpec((1, d, tf), lambda i, j, eid, nu: (eid[i], 0, jj(i, j, nu))),
                  pl.BlockSpec((1, tf, d), lambda i, j, eid, nu: (eid[i], jj(i, j, nu), 0))],
        out_specs=pl.BlockSpec((tm * nchunk, LANES), lambda i, j, eid, nu: (i, 0)),
        scratch_shapes=[pltpu.VMEM((tm, d), BF16), pltpu.VMEM((tm, d), F32)],
    )
    return pl.pallas_call(
        _experts_kernel,
        out_shape=jax.ShapeDtypeStruct(x.shape, F32),
        grid_spec=grid_spec,
        compiler_params=_cparams(("parallel", "arbitrary")),
        name="ffn_experts",
    )(eid, nused, x, g, wg, wu, wd)


def _lru_kernel(xb_ref, yb_ref, cw_ref, cb_ref, wr_ref, br_ref, wi_ref, bi_ref, lam_ref,
                o_ref, xext_ref, h_ref):
    tb, w = xb_ref.shape

    @pl.when(pl.program_id(1) == 0)
    def _():
        xext_ref[0:SUBLANES, :] = jnp.zeros((SUBLANES, w), F32)
        h_ref[...] = jnp.zeros_like(h_ref)

    xext_ref[SUBLANES:SUBLANES + tb, :] = xb_ref[...].astype(F32)
    cw = cw_ref[...]
    xc = cb_ref[...] + cw[CONV_W - 1:CONV_W, :] * xext_ref[SUBLANES:SUBLANES + tb, :]
    for j in range(CONV_W - 1):
        xc = xc + cw[j:j + 1, :] * xext_ref[pl.ds(SUBLANES - (CONV_W - 1) + j, tb), :]
    xext_ref[0:SUBLANES, :] = xext_ref[tb:tb + SUBLANES, :]

    xcb = xc.astype(BF16)
    nb = wr_ref.shape[0]
    bw = w // nb
    r_parts, i_parts = [], []
    for n in range(nb):
        xs = xcb[:, n * bw:(n + 1) * bw]
        r_parts.append(jnp.dot(xs, wr_ref[n], preferred_element_type=F32))
        i_parts.append(jnp.dot(xs, wi_ref[n], preferred_element_type=F32))
    r = _sigmoid(jnp.concatenate(r_parts, axis=1) + br_ref[...])
    ig = _sigmoid(jnp.concatenate(i_parts, axis=1) + bi_ref[...])

    neg_lam = -lam_ref[...]
    softplus = jnp.maximum(neg_lam, 0.0) + jnp.log1p(jnp.exp(-jnp.abs(neg_lam)))
    log_a = -LRU_C * r * softplus
    a = jnp.exp(log_a)
    u = jnp.sqrt(1.0 - a * a) * (ig * xc)

    row = lax.broadcasted_iota(I32, (tb, w), 0)
    s = 1
    while s < tb:
        keep = row >= s
        a_sh = pltpu.roll(a, s, 0)
        u_sh = pltpu.roll(u, s, 0)
        u = jnp.where(keep, a * u_sh + u, u)
        a = jnp.where(keep, a * a_sh, a)
        s *= 2
    h = a * h_ref[0:1, :] + u
    h_ref[0:1, :] = h[tb - 1:tb, :]

    y = yb_ref[...].astype(F32)
    gelu = 0.5 * y * (1.0 + jnp.tanh(0.7978845608028654 * (y + 0.044715 * (y * y * y))))
    o_ref[...] = (h * gelu).astype(o_ref.dtype)


def rg_lru(z, conv_w, conv_b, w_r, b_r, w_i, b_i, lam):
    b, tp, _ = z.shape
    w = conv_w.shape[1]
    tb = TIME_TILE
    nb = w_r.shape[0]
    bw = w // nb
    full = lambda shape: pl.BlockSpec(shape, lambda bi, ti: (0,) * len(shape))
    return pl.pallas_call(
        _lru_kernel,
        out_shape=jax.ShapeDtypeStruct((b, tp, w), BF16),
        grid=(b, tp // tb),
        in_specs=[pl.BlockSpec((None, tb, w), lambda bi, ti: (bi, ti, 0)),
                  pl.BlockSpec((None, tb, w), lambda bi, ti: (bi, ti, 1)),
                  full((CONV_W, w)), full((1, w)),
                  full((nb, bw, bw)), full((1, w)),
                  full((nb, bw, bw)), full((1, w)),
                  full((1, w))],
        out_specs=pl.BlockSpec((None, tb, w), lambda bi, ti: (bi, ti, 0)),
        scratch_shapes=[pltpu.VMEM((tb + 2 * SUBLANES, w), F32), pltpu.VMEM((SUBLANES, w), F32)],
        compiler_params=_cparams(("parallel", "arbitrary")),
        name="rg_lru",
    )(z, z, conv_w, conv_b, w_r, b_r, w_i, b_i, lam)


def _gla_kernel(q_ref, k_ref, v_ref, g_ref, glr_ref, wgk_ref, bgk_ref, ng_ref, o_ref, st_ref,
                *, chunk, heads):
    tb, hdk = q_ref.shape
    dk = hdk // heads
    dv = v_ref.shape[1] // heads
    c = chunk

    @pl.when(pl.program_id(1) == 0)
    def _():
        st_ref[...] = jnp.zeros_like(st_ref)

    row = lax.broadcasted_iota(I32, (c, hdk), 0)
    tri = lax.broadcasted_iota(I32, (c, c), 0) >= lax.broadcasted_iota(I32, (c, c), 1)
    nt = (((1,), (1,)), ((), ()))
    tn = (((0,), (0,)), ((), ()))

    def body(ci, carry):
        sl = pl.ds(pl.multiple_of(ci * c, c), c)
        pre = jnp.dot(glr_ref[sl, :], wgk_ref[...], preferred_element_type=F32) + bgk_ref[...]
        gk = (jnp.minimum(pre, 0.0) - jnp.log1p(jnp.exp(-jnp.abs(pre)))) * (1.0 / GLA_GATE_NORM)
        bc = gk
        s = 1
        while s < c:
            bc = bc + jnp.where(row >= s, pltpu.roll(bc, s, 0), 0.0)
            s *= 2
        q = q_ref[sl, :].astype(F32) * (dk ** -0.5)
        k = k_ref[sl, :].astype(F32)
        v = v_ref[sl, :]
        g = g_ref[sl, :].astype(F32)
        bmid = bc[c // 2:c // 2 + 1, :]
        blast = bc[c - 1:c, :]
        q_inter = (q * jnp.exp(bc)).astype(BF16)
        q_intra = (q * jnp.exp(bc - bmid)).astype(BF16)
        k_intra = (k * jnp.exp(bmid - bc)).astype(BF16)
        k_state = (k * jnp.exp(blast - bc)).astype(BF16)
        decay = jnp.exp(blast)
        outs = []
        for h in range(heads):
            ks = slice(h * dk, (h + 1) * dk)
            vs = slice(h * dv, (h + 1) * dv)
            st = st_ref[h]
            vh = v[:, vs]
            inter = lax.dot_general(q_inter[:, ks], st.astype(BF16), nt, preferred_element_type=F32)
            att = lax.dot_general(q_intra[:, ks], k_intra[:, ks], nt, preferred_element_type=F32)
            att = jnp.where(tri, att, 0.0).astype(BF16)
            o = inter + jnp.dot(att, vh, preferred_element_type=F32)
            st_ref[h] = st * decay[:, ks] + lax.dot_general(vh, k_state[:, ks], tn,
                                                           preferred_element_type=F32)
            o = _rms(o, ng_ref[...])
            gh = g[:, vs]
            outs.append(o * (gh * _sigmoid(gh)))
        o_ref[sl, :] = jnp.concatenate(outs, axis=1).astype(o_ref.dtype)
        return carry

    lax.fori_loop(0, tb // c, body, 0)


def gla(z, col0, w_gk, b_gk, norm_g, hdk, hdv):
    b, tp, _ = z.shape
    tb = TIME_TILE
    cq, ck = col0 // hdk, col0 // hdk + 1
    cv = (col0 + 2 * hdk) // hdv
    cg = cv + 1
    cl = (col0 + 2 * hdk + 2 * hdv) // LANES
    full = lambda shape: pl.BlockSpec(shape, lambda bi, ti: (0,) * len(shape))
    return pl.pallas_call(
        functools.partial(_gla_kernel, chunk=GLA_CHUNK, heads=GLA_HEADS),
        out_shape=jax.ShapeDtypeStruct((b, tp, hdv), BF16),
        grid=(b, tp // tb),
        in_specs=[pl.BlockSpec((None, tb, hdk), lambda bi, ti: (bi, ti, cq)),
                  pl.BlockSpec((None, tb, hdk), lambda bi, ti: (bi, ti, ck)),
                  pl.BlockSpec((None, tb, hdv), lambda bi, ti: (bi, ti, cv)),
                  pl.BlockSpec((None, tb, hdv), lambda bi, ti: (bi, ti, cg)),
                  pl.BlockSpec((None, tb, LANES), lambda bi, ti: (bi, ti, cl)),
                  full((LANES, hdk)), full((1, hdk)), full((1, hdv // GLA_HEADS))],
        out_specs=pl.BlockSpec((None, tb, hdv), lambda bi, ti: (bi, ti, 0)),
        scratch_shapes=[pltpu.VMEM((GLA_HEADS, hdv // GLA_HEADS, hdk // GLA_HEADS), F32)],
        compiler_params=_cparams(("parallel", "arbitrary")),
        name="gla",
    )(z, z, z, z, z, w_gk, b_gk, norm_g)


def _rope_tile(x, cos, sin_up, sin_dn, half):
    return x * cos + pltpu.roll(x, half, 1) * sin_up + pltpu.roll(x, LANES - half, 1) * sin_dn


def _rope_kernel(qk_ref, qi_ref, kw_ref, tab_ref, qk_out, qi_out, ki_out, wi_out):
    def tabs(kind):
        return tab_ref[3 * kind], tab_ref[3 * kind + 1], tab_ref[3 * kind + 2]

    n_q_tiles = ATT_HEADS
    for t in range(qk_ref.shape[1] // LANES):
        ca, sa_up, sa_dn = tabs(0 if t < n_q_tiles else 1)
        sl = slice(t * LANES, (t + 1) * LANES)
        qk_out[:, sl] = _rope_tile(qk_ref[:, sl].astype(F32), ca, sa_up, sa_dn, 16).astype(BF16)
    cb, sb_up, sb_dn = tabs(2)
    for t in range(qi_ref.shape[1] // LANES):
        sl = slice(t * LANES, (t + 1) * LANES)
        qi_out[:, sl] = _rope_tile(qi_ref[:, sl].astype(F32), cb, sb_up, sb_dn, 8).astype(BF16)
    ck, sk_up, sk_dn = tabs(3)
    kw = _rope_tile(kw_ref[...].astype(F32), ck, sk_up, sk_dn, 8)
    ki_out[...] = kw.astype(BF16)
    wi_out[...] = pltpu.roll(kw, LANES - IDX_DIM, 1)


def rope_tables(tp, head_dim):
    pos = jnp.arange(tp, dtype=F32)[:, None]
    lane = jnp.arange(LANES)

    def tables(dim, period, active):
        half = (dim // 4) // 2
        freqs = ROPE_THETA ** (-jnp.arange(half, dtype=F32) / half)
        ang = pos * freqs[None, :]
        cos, sin = jnp.cos(ang), jnp.sin(ang)
        within = lane % period
        fidx = within % half
        cos_l, sin_l = cos[:, fidx], sin[:, fidx]
        lo = (within < half) & active
        hi = (within >= half) & (within < 2 * half) & active
        c = jnp.where((lo | hi)[None, :], cos_l, 1.0)
        s_up = jnp.where(hi[None, :], sin_l, 0.0)
        s_dn = jnp.where(lo[None, :], -sin_l, 0.0)
        return c, s_up, s_dn

    all_on = jnp.ones((LANES,), bool)
    a = tables(head_dim, head_dim, all_on)
    bq = tables(IDX_DIM, IDX_DIM, all_on)
    ck, sk_up, sk_dn = tables(IDX_DIM, IDX_DIM, lane < IDX_DIM)
    wscale = IDX_HEADS ** -0.5 * IDX_DIM ** -0.5
    ck = jnp.where((lane >= IDX_DIM)[None, :],
                   jnp.where(lane < IDX_DIM + IDX_HEADS, wscale, 0.0)[None, :], ck)
    a_scaled = [t * (head_dim ** -0.5 * LOG2_E) for t in a]
    return jnp.stack([*a_scaled, *a, *bq, ck, sk_up, sk_dn])


def rope(z, tabs, tp, nqk, nqi, col_qi, col_kw):
    m = z.shape[0]
    tm = TIME_TILE
    nper = tp // tm
    return pl.pallas_call(
        _rope_kernel,
        out_shape=[jax.ShapeDtypeStruct((m, nqk), BF16), jax.ShapeDtypeStruct((m, nqi), BF16),
                   jax.ShapeDtypeStruct((m, LANES), BF16), jax.ShapeDtypeStruct((m, LANES), F32)],
        grid=(m // tm,),
        in_specs=[pl.BlockSpec((tm, nqk), lambda i: (i, 0)),
                  pl.BlockSpec((tm, nqi), lambda i: (i, col_qi // nqi)),
                  pl.BlockSpec((tm, LANES), lambda i: (i, col_kw // LANES)),
                  pl.BlockSpec((12, tm, LANES), lambda i: (0, i % nper, 0))],
        out_specs=[pl.BlockSpec((tm, nqk), lambda i: (i, 0)),
                   pl.BlockSpec((tm, nqi), lambda i: (i, 0)),
                   pl.BlockSpec((tm, LANES), lambda i: (i, 0)),
                   pl.BlockSpec((tm, LANES), lambda i: (i, 0))],
        compiler_params=_cparams(("parallel",)),
        name="rope",
    )(z, z, z, tabs)


def _index_keys(qi_stack, w_rows, ki_blk, q0, k0):
    nt = (((1,), (1,)), ((), ()))
    tq = w_rows[0].shape[1]
    lg = lax.dot_general(ki_blk, qi_stack, nt, preferred_element_type=F32)
    score = None
    for h, wh in enumerate(w_rows):
        term = wh * jnp.maximum(lg[:, h * tq:(h + 1) * tq], 0.0)
        score = term if score is None else score + term
    tk = score.shape[0]
    kpos = k0 + lax.broadcasted_iota(I32, (tk, tq), 0)
    qpos = q0 + lax.broadcasted_iota(I32, (tk, tq), 1)
    score = jnp.where(kpos <= qpos, jnp.where(kpos < N_META_TOK, BIG, score), NEG)
    bits = pltpu.bitcast(score, I32)
    return jnp.where(bits < 0, bits ^ jnp.int32(0x7FFFFFFF), bits)


def _float_key(x):
    bits = int(np.float32(x).view(np.int32))
    return bits ^ 0x7FFFFFFF if bits < 0 else bits


def _indexer_kernel(qi_ref, wi_ref, ki_ref, bias_ref, keys_ref, qs_ref, cut_ref, *, k_sel, tk):
    tq = qi_ref.shape[0]
    qb = pl.program_id(1)
    q0 = qb * tq
    nkv = (q0 + tq + tk - 1) // tk
    for h in range(IDX_HEADS):
        qs_ref[h * tq:(h + 1) * tq, :] = qi_ref[:, h * IDX_DIM:(h + 1) * IDX_DIM]
    w_t = jnp.transpose(wi_ref[...])
    w_rows = [w_t[h:h + 1, :] for h in range(IDX_HEADS)]

    def fill(kb, carry):
        k0 = pl.multiple_of(kb * tk, tk)
        keys_ref[pl.ds(k0, tk), :] = _index_keys(qs_ref[...], w_rows,
                                                 ki_ref[pl.ds(k0, tk), 0:IDX_DIM], q0, k0)
        return carry

    lax.fori_loop(0, nkv, fill, 0)

    row = lax.broadcasted_iota(I32, (tk, tq), 0)

    def block(kb):
        k0 = pl.multiple_of(kb * tk, tk)
        return keys_ref[pl.ds(k0, tk), :], k0 + row

    def count(pred):
        def blk(kb, acc):
            hit = jnp.where(pred(*block(kb)), 1, 0)
            return acc + jnp.sum(hit.reshape(tk // SUBLANES, SUBLANES, tq), axis=0)
        acc = lax.fori_loop(0, nkv, blk, jnp.zeros((SUBLANES, tq), I32))
        return jnp.sum(acc, axis=0, keepdims=True)

    def bit_step(bi, thr):
        cand = thr + jnp.left_shift(jnp.int32(1), 31 - bi)
        n = count(lambda keys, kpos: keys >= cand)
        return jnp.where(n >= k_sel, cand, thr)

    thr = lax.fori_loop(0, 32, bit_step, jnp.full((1, tq), INT_MIN, I32))
    n_ge = count(lambda keys, kpos: keys >= thr)
    no_tie_cut = jnp.full((1, tq), 1 << 30, I32)
    cut_ref[...] = no_tie_cut

    @pl.when(jnp.max(n_ge) > k_sel)
    def _():
        n_gt = count(lambda keys, kpos: keys > thr)
        need = k_sel - n_gt

        def cut_step(bi, cut):
            cand = cut + jnp.left_shift(jnp.int32(1), 15 - bi)
            n = count(lambda keys, kpos: (keys == thr) & (kpos < cand))
            return jnp.where(n <= need, cand, cut)

        cut = lax.fori_loop(0, 16, cut_step, jnp.zeros((1, tq), I32))
        cut_ref[...] = jnp.where(n_ge > k_sel, cut, no_tie_cut)

    bias_ref[...] = jnp.full(bias_ref.shape, NEG, BF16)
    cut = cut_ref[...]
    key_valid = jnp.int32(_float_key(0.5 * NEG))

    def emit(kb, carry):
        k0 = pl.multiple_of(kb * tk, tk)
        keys, kpos = block(kb)
        chosen = ((keys > thr) | ((keys == thr) & (kpos < cut))) & (keys > key_valid)
        bias_ref[pl.ds(k0, tk), :] = jnp.where(chosen, 0.0, NEG).astype(BF16)
        return carry

    lax.fori_loop(0, nkv, emit, 0)


def indexer(qi, wi, ki, b, tp, k_sel):
    tq = INDEX_Q_TILE
    nq = tp // tq
    return pl.pallas_call(
        functools.partial(_indexer_kernel, k_sel=k_sel, tk=TIME_TILE),
        out_shape=jax.ShapeDtypeStruct((b * tp, tp), BF16),
        grid=(b, nq),
        in_specs=[pl.BlockSpec((tq, qi.shape[1]), lambda bi, qb: (bi * nq + qb, 0)),
                  pl.BlockSpec((tq, LANES), lambda bi, qb: (bi * nq + qb, 0)),
                  pl.BlockSpec((tp, LANES), lambda bi, qb: (bi, 0))],
        out_specs=pl.BlockSpec((tp, tq), lambda bi, qb: (bi, qb)),
        scratch_shapes=[pltpu.VMEM((tp, tq), I32), pltpu.VMEM((IDX_HEADS * tq, IDX_DIM), BF16),
                        pltpu.VMEM((1, tq), I32)],
        compiler_params=_cparams(("parallel", "arbitrary")),
        name="indexer",
    )(qi, wi, ki)


def _attn_kernel(qb_ref, kb_ref, q_ref, k_ref, v_ref, bias_ref, o_ref, qs_ref, m_ref, l_ref, acc_ref):
    tq = q_ref.shape[0]
    tk = k_ref.shape[0]
    hd = k_ref.shape[1] // ATT_KV_HEADS
    grows = ATT_GROUPS * tq
    qb = qb_ref[pl.program_id(1)]
    kb = kb_ref[pl.program_id(1)]
    nt = (((1,), (1,)), ((), ()))

    @pl.when(kb == 0)
    def _():
        m_ref[...] = jnp.full_like(m_ref, NEG)
        l_ref[...] = jnp.zeros_like(l_ref)
        acc_ref[...] = jnp.zeros_like(acc_ref)
        for hi in range(ATT_HEADS):
            qs_ref[hi * tq:(hi + 1) * tq, :] = q_ref[:, hi * hd:(hi + 1) * hd]

    bias = jnp.transpose(bias_ref[...].astype(F32))[None]
    ones = jnp.ones((tk, LANES), BF16)
    for kvh in range(ATT_KV_HEADS):
        rows = slice(kvh * grows, (kvh + 1) * grows)
        kh = k_ref[:, kvh * hd:(kvh + 1) * hd]
        vh1 = jnp.concatenate([v_ref[:, kvh * hd:(kvh + 1) * hd], ones], axis=1)
        s = lax.dot_general(qs_ref[rows, :], kh, nt, preferred_element_type=F32)
        s = (s.reshape(ATT_GROUPS, tq, tk) + bias).reshape(grows, tk)
        m_old = m_ref[rows, :]
        m_new = jnp.maximum(m_old, jnp.broadcast_to(jnp.max(s, axis=1, keepdims=True),
                                                    (grows, LANES)))
        p = jnp.concatenate(
            [jnp.exp2(s[:, t * LANES:(t + 1) * LANES] - m_new).astype(BF16)
             for t in range(tk // LANES)], axis=1)
        alpha = jnp.exp2(m_old - m_new)
        pv = jnp.dot(p, vh1, preferred_element_type=F32)
        l_ref[rows, :] = alpha * l_ref[rows, :] + pv[:, hd:]
        acc_ref[rows, :] = alpha * acc_ref[rows, :] + pv[:, :hd]
        m_ref[rows, :] = m_new

    @pl.when(kb == qb)
    def _():
        for hi in range(ATT_HEADS):
            rows = slice(hi * tq, (hi + 1) * tq)
            o_ref[:, hi * hd:(hi + 1) * hd] = (acc_ref[rows, :] / l_ref[rows, :]).astype(o_ref.dtype)


def sparse_attention(qk, z, bias, b, tp, d, col_v):
    tq = tk = TIME_TILE
    nq = tp // tq
    hd = d // ATT_HEADS
    kvw = ATT_KV_HEADS * hd
    pairs = [(qb, kb) for qb in range(nq) for kb in range(qb + 1)]
    qb_of = jnp.asarray([p[0] for p in pairs], I32)
    kb_of = jnp.asarray([p[1] for p in pairs], I32)
    qrow = lambda bi, i, qbs, kbs: bi * nq + qbs[i]
    kvrow = lambda bi, i, qbs, kbs: bi * nq + kbs[i]
    grid_spec = pltpu.PrefetchScalarGridSpec(
        num_scalar_prefetch=2,
        grid=(b, len(pairs)),
        in_specs=[pl.BlockSpec((tq, d), lambda bi, i, qbs, kbs: (qrow(bi, i, qbs, kbs), 0)),
                  pl.BlockSpec((tk, kvw), lambda bi, i, qbs, kbs: (kvrow(bi, i, qbs, kbs), d // kvw)),
                  pl.BlockSpec((tk, kvw),
                               lambda bi, i, qbs, kbs: (kvrow(bi, i, qbs, kbs), col_v // kvw)),
                  pl.BlockSpec((tk, tq), lambda bi, i, qbs, kbs: (kvrow(bi, i, qbs, kbs), qbs[i]))],
        out_specs=pl.BlockSpec((tq, d), lambda bi, i, qbs, kbs: (qrow(bi, i, qbs, kbs), 0)),
        scratch_shapes=[pltpu.VMEM((ATT_HEADS * tq, hd), BF16),
                        pltpu.VMEM((ATT_HEADS * tq, LANES), F32),
                        pltpu.VMEM((ATT_HEADS * tq, LANES), F32),
                        pltpu.VMEM((ATT_HEADS * tq, hd), F32)],
    )
    return pl.pallas_call(
        _attn_kernel,
        out_shape=jax.ShapeDtypeStruct((b * tp, d), BF16),
        grid_spec=grid_spec,
        compiler_params=_cparams(("parallel", "arbitrary")),
        name="sparse_attention",
    )(qb_of, kb_of, qk, qk, z, bias)


def _router_kernel(x_ref, g_ref, w_ref, o_ref, xt_ref):
    tm, d = x_ref.shape
    nchunk = d // LANES
    for c in range(nchunk):
        xt_ref[pl.ds(c, tm, stride=nchunk), :] = x_ref[:, c * LANES:(c + 1) * LANES]
    xn = _rms(x_ref[...], g_ref[...]).astype(BF16)
    logits = jnp.dot(xn, w_ref[...], preferred_element_type=F32)
    lane = lax.broadcasted_iota(I32, logits.shape, 1)
    logits = jnp.where(lane < N_EXPERTS, logits, -jnp.inf)
    v1 = jnp.max(logits, axis=1, keepdims=True)
    e1 = jnp.min(jnp.where(logits == v1, lane, LANES), axis=1, keepdims=True)
    rest = jnp.where(lane == e1, -jnp.inf, logits)
    v2 = jnp.max(rest, axis=1, keepdims=True)
    e2 = jnp.min(jnp.where(rest == v2, lane, LANES), axis=1, keepdims=True)
    ex = jnp.exp(v2 - v1)
    g1 = 1.0 / (1.0 + ex)
    g2 = ex / (1.0 + ex)
    out = jnp.where(lane == 0, g1, jnp.where(lane == 1, g2, 0.0))
    out = jnp.where(lane == 2, e1.astype(F32), jnp.where(lane == 3, e2.astype(F32), out))
    o_ref[...] = out


def router(x, g, w):
    m, d = x.shape
    tm = TIME_TILE
    nchunk = d // LANES
    return pl.pallas_call(
        _router_kernel,
        out_shape=[jax.ShapeDtypeStruct((m, LANES), F32),
                   jax.ShapeDtypeStruct((m * nchunk, LANES), F32)],
        grid=(m // tm,),
        in_specs=[pl.BlockSpec((tm, d), lambda i: (i, 0)),
                  pl.BlockSpec((1, d), lambda i: (0, 0)),
                  pl.BlockSpec((d, LANES), lambda i: (0, 0))],
        out_specs=[pl.BlockSpec((tm, LANES), lambda i: (i, 0)),
                   pl.BlockSpec((tm * nchunk, LANES), lambda i: (i, 0))],
        compiler_params=_cparams(("parallel",)),
        name="router",
    )(x, g, w)


def _gather_kernel(idx_ref, src_ref, dst_ref, sem, *, rows, nchunk):
    base = pl.program_id(0) * rows

    def token_copy(r):
        src_row = pl.multiple_of(idx_ref[base + r] * nchunk, nchunk)
        dst_row = pl.multiple_of(r * nchunk, nchunk)
        return pltpu.make_async_copy(src_ref.at[pl.ds(src_row, nchunk)],
                                     dst_ref.at[pl.ds(dst_row, nchunk)], sem)

    def start(r, carry):
        token_copy(r).start()
        return carry

    def wait(r, carry):
        token_copy(r).wait()
        return carry

    lax.fori_loop(0, rows, start, 0)
    lax.fori_loop(0, rows, wait, 0)


def gather_tokens(src, idx, nchunk):
    n = idx.shape[0]
    rows = _pick_tile(n, (GATHER_ROWS, TIME_TILE, 256, 128))
    grid_spec = pltpu.PrefetchScalarGridSpec(
        num_scalar_prefetch=1,
        grid=(n // rows,),
        in_specs=[pl.BlockSpec(memory_space=pl.ANY)],
        out_specs=pl.BlockSpec((rows * nchunk, LANES), lambda i, idx: (i, 0)),
        scratch_shapes=[pltpu.SemaphoreType.DMA(())],
    )
    return pl.pallas_call(
        functools.partial(_gather_kernel, rows=rows, nchunk=nchunk),
        out_shape=jax.ShapeDtypeStruct((n * nchunk, LANES), src.dtype),
        grid_spec=grid_spec,
        compiler_params=_cparams(("arbitrary",)),
        name="gather_tokens",
    )(idx, src)


def _combine_norm_kernel(h_ref, y1_ref, y2_ref, route_ref, g_ref, o_ref):
    tm, d = h_ref.shape
    nchunk = d // LANES
    g1 = route_ref[:, 0:1]
    g2 = route_ref[:, 1:2]
    ssq = jnp.zeros((tm, LANES), F32)
    for c in range(nchunk):
        sl = slice(c * LANES, (c + 1) * LANES)
        hc = (h_ref[:, sl] + g1 * _chunk_rows(y1_ref, c, tm, nchunk)
              + g2 * _chunk_rows(y2_ref, c, tm, nchunk))
        o_ref[:, sl] = hc
        ssq = ssq + hc * hc
    inv = lax.rsqrt(jnp.sum(ssq, axis=1, keepdims=True) / d + EPS)
    o_ref[...] = o_ref[...] * inv * g_ref[...]


def combine_norm(h, y, route, g):
    m, d = h.shape
    tm = TIME_TILE
    nblk = m // tm
    nchunk = d // LANES
    return pl.pallas_call(
        _combine_norm_kernel,
        out_shape=jax.ShapeDtypeStruct((m, d), F32),
        grid=(nblk,),
        in_specs=[pl.BlockSpec((tm, d), lambda i: (i, 0)),
                  pl.BlockSpec((tm * nchunk, LANES), lambda i: (i, 0)),
                  pl.BlockSpec((tm * nchunk, LANES), lambda i: (i + nblk, 0)),
                  pl.BlockSpec((tm, LANES), lambda i: (i, 0)),
                  pl.BlockSpec((1, d), lambda i: (0, 0))],
        out_specs=pl.BlockSpec((tm, d), lambda i: (i, 0)),
        compiler_params=_cparams(("parallel",)),
        name="combine_norm",
    )(h, y, y, route, g)


def _pad_cols(w, n):
    return jnp.pad(w, ((0, 0), (0, n - w.shape[1])))


def _round_up(n, k):
    return -(-n // k) * k


def layer_ab(h, b, tp, norm_g, w_in, conv_w, conv_b, w_r, b_r, w_i, b_i, lam, w_gk, b_gk,
             gla_norm, w_out, ffn_norm, ffn_wg, ffn_wu, ffn_wd):
    m, d = h.shape
    lw = conv_w.shape[1]
    hdk = w_gk.shape[1]
    hdv = d - lw
    n_in = _round_up(w_in.shape[1] - GLA_RANK + LANES, COL_TILE)
    z = norm_matmul(h, norm_g[None], _pad_cols(w_in, n_in).astype(BF16)).reshape(b, tp, n_in)
    a_out = rg_lru(z, conv_w, conv_b[None], w_r.astype(BF16), b_r.reshape(1, lw),
                   w_i.astype(BF16), b_i.reshape(1, lw), lam[None])
    w_gk_p = jnp.pad(w_gk, ((0, LANES - GLA_RANK), (0, 0))).astype(BF16)
    b_out = gla(z, 2 * lw, w_gk_p, b_gk[None], gla_norm[None], hdk, hdv)
    h = matmul_residual([a_out.reshape(m, lw), b_out.reshape(m, hdv)], w_out.astype(BF16), h)
    return ffn_residual(h, ffn_norm[None], ffn_wg.astype(BF16)[None], ffn_wu.astype(BF16)[None],
                        ffn_wd.astype(BF16)[None])


def layer_c(h, b, tp, t_real, norm_g, w_in, w_out):
    m, d = h.shape
    hd = d // ATT_HEADS
    kvw = ATT_KV_HEADS * hd
    nqi = IDX_HEADS * IDX_DIM
    col_v = d + kvw
    col_qi = d + 2 * kvw
    col_kw = col_qi + nqi
    n_in = _round_up(col_kw + LANES, COL_TILE)
    z = norm_matmul(h, norm_g[None], _pad_cols(w_in, n_in).astype(BF16))
    qk, qi, ki, wi = rope(z, rope_tables(tp, hd), tp, d + kvw, nqi, col_qi, col_kw)
    k_sel = min(TOPK_MAX, (t_real - N_META_TOK) // 4)
    bias = indexer(qi, wi, ki, b, tp, k_sel)
    att = sparse_attention(qk, z, bias, b, tp, d, col_v)
    return matmul_residual([att], w_out.astype(BF16), h)


def moe_dispatch(route, tm):
    m = route.shape[0]
    e_flat = route[:, 2:4].astype(I32).reshape(-1)
    onehot = (e_flat[:, None] == jnp.arange(N_EXPERTS)[None, :]).astype(I32)
    rank = jnp.take_along_axis(jnp.cumsum(onehot, axis=0) - onehot, e_flat[:, None], axis=1)[:, 0]
    counts = jnp.sum(onehot, axis=0)
    padded = (counts + tm - 1) // tm * tm
    pad_end = jnp.cumsum(padded)
    dest = (pad_end - padded)[e_flat] + rank
    nblk = -(-(2 * m) // tm) + N_EXPERTS
    slot_tok = jnp.zeros((nblk * tm,), I32).at[dest].set(jnp.arange(2 * m, dtype=I32) // 2)
    nused = (pad_end[-1] // tm).astype(I32)
    blk = jnp.minimum(jnp.arange(nblk, dtype=I32), nused - 1)
    blk_expert = jnp.minimum(jnp.sum((pad_end[None, :] <= (blk * tm)[:, None]).astype(I32), axis=1),
                             N_EXPERTS - 1)
    return slot_tok, dest.reshape(m, 2), blk_expert, nused.reshape(1)


def layer_moe_final(h, norm_g, w_router, wg, wu, wd, final_g):
    m, d = h.shape
    nchunk = d // LANES
    route, h_tok = router(h, norm_g[None], _pad_cols(w_router, LANES).astype(BF16))
    tm = _pick_tile(m, (ROW_TILE_BIG, TIME_TILE))
    slot_tok, tok_slots, blk_expert, nused = moe_dispatch(route, tm)
    x_slots = gather_tokens(h_tok, slot_tok, nchunk)
    y_slots = ffn_experts(x_slots, norm_g[None], wg.astype(BF16), wu.astype(BF16), wd.astype(BF16),
                          blk_expert, nused, tm)
    y_tok = gather_tokens(y_slots, tok_slots.T.reshape(-1), nchunk)
    return combine_norm(h, y_tok, route, final_g[None])


def kernel(x, meta, ab_norm, ab_w_in, lru_conv_w, lru_conv_b, lru_w_r, lru_b_r, lru_w_i, lru_b_i,
           lru_lam, gla_w_gk, gla_b_gk, gla_norm, ab_w_out, ffn_norm, ffn_w_gate, ffn_w_up,
           ffn_w_down, c_norm, c_w_in, c_w_out, moe_norm, moe_router, moe_w_gate, moe_w_up,
           moe_w_down, final_norm):
    b, seq, d = x.shape
    assert ab_norm.shape[0] == 1 and c_norm.shape[0] == 1, "two-layer trunk only"
    t_real = seq + N_META_TOK
    tp = _round_up(t_real, TIME_TILE)
    h = jnp.concatenate([jnp.broadcast_to(meta.astype(x.dtype)[None], (b, N_META_TOK, d)), x,
                         jnp.zeros((b, tp - t_real, d), x.dtype)], axis=1).reshape(b * tp, d)
    h = layer_ab(h, b, tp, ab_norm[0], ab_w_in[0], lru_conv_w[0], lru_conv_b[0], lru_w_r[0],
                 lru_b_r[0], lru_w_i[0], lru_b_i[0], lru_lam[0], gla_w_gk[0], gla_b_gk[0],
                 gla_norm[0], ab_w_out[0], ffn_norm[0], ffn_w_gate[0], ffn_w_up[0], ffn_w_down[0])
    h = layer_c(h, b, tp, t_real, c_norm[0], c_w_in[0], c_w_out[0])
    out = layer_moe_final(h, moe_norm[0], moe_router[0], moe_w_gate[0], moe_w_up[0],
                          moe_w_down[0], final_norm)
    return out.reshape(b, tp, d)[:, N_META_TOK:t_real]
```

```python
import functools

import jax
import jax.numpy as jnp
import numpy as np
from jax import lax
from jax.experimental import pallas as pl
from jax.experimental.pallas import tpu as pltpu

F32 = jnp.float32
BF16 = jnp.bfloat16
I32 = jnp.int32

N_META_TOK = 16
EPS = 1e-6
ROPE_THETA = 500000.0
LRU_BLOCKS = 4
CONV_W = 4
LRU_C = 8.0
GLA_HEADS = 4
GLA_RANK = 16
GLA_GATE_NORM = 16.0
ATT_HEADS = 16
ATT_KV_HEADS = 4
ATT_GROUPS = ATT_HEADS // ATT_KV_HEADS
IDX_HEADS = 8
IDX_DIM = 64
TOPK_MAX = 256
BIG = 1e30
NEG = -1e30
N_EXPERTS = 8
LOG2_E = 1.4426950408889634

LANES = 128
SUBLANES = 8
BF16_ROWS = 16
TIME_TILE = 384
INDEX_Q_TILE = 128
ROW_TILE_BIG = 768
COL_TILE = 512
GLA_CHUNK = 64
GATHER_ROWS = 512
VMEM_LIMIT = 56 * 1024 * 1024
INT_MIN = -(2 ** 31)


def _cparams(sem):
    return pltpu.CompilerParams(dimension_semantics=sem, vmem_limit_bytes=VMEM_LIMIT)


def _rms(x, g):
    return x * lax.rsqrt(jnp.mean(x * x, axis=-1, keepdims=True) + EPS) * g


def _sigmoid(x):
    return 1.0 / (1.0 + jnp.exp(-x))


def _pick_tile(n, candidates):
    for c in candidates:
        if n % c == 0:
            return c
    raise ValueError(f"no tile in {candidates} divides {n}")


def _norm_matmul_kernel(x_ref, g_ref, w_ref, o_ref, xn_ref):
    @pl.when(pl.program_id(1) == 0)
    def _():
        xn_ref[...] = _rms(x_ref[...], g_ref[...]).astype(BF16)

    o_ref[...] = jnp.dot(xn_ref[...], w_ref[...], preferred_element_type=F32).astype(o_ref.dtype)


def norm_matmul(x, g, w):
    m, d = x.shape
    n = w.shape[1]
    tm = _pick_tile(m, (ROW_TILE_BIG, TIME_TILE))
    tn = COL_TILE
    return pl.pallas_call(
        _norm_matmul_kernel,
        out_shape=jax.ShapeDtypeStruct((m, n), BF16),
        grid=(m // tm, n // tn),
        in_specs=[pl.BlockSpec((tm, d), lambda i, j: (i, 0)),
                  pl.BlockSpec((1, d), lambda i, j: (0, 0)),
                  pl.BlockSpec((d, tn), lambda i, j: (0, j))],
        out_specs=pl.BlockSpec((tm, tn), lambda i, j: (i, j)),
        scratch_shapes=[pltpu.VMEM((tm, d), BF16)],
        compiler_params=_cparams(("parallel", "arbitrary")),
        name="norm_matmul",
    )(x, g, w)


def _matmul_res_kernel(*refs, nparts):
    xs = refs[:nparts]
    w_ref, r_ref, o_ref = refs[nparts:]
    acc = r_ref[...]
    k0 = 0
    for x_ref in xs:
        kp = x_ref.shape[1]
        acc = acc + jnp.dot(x_ref[...], w_ref[k0:k0 + kp, :], preferred_element_type=F32)
        k0 += kp
    o_ref[...] = acc


def matmul_residual(parts, w, res):
    m, n = res.shape
    k = w.shape[0]
    tm = _pick_tile(m, (ROW_TILE_BIG, TIME_TILE))
    tn = 1024
    in_specs = [pl.BlockSpec((tm, p.shape[1]), lambda i, j: (i, 0)) for p in parts]
    in_specs += [pl.BlockSpec((k, tn), lambda i, j: (0, j)),
                 pl.BlockSpec((tm, tn), lambda i, j: (i, j))]
    return pl.pallas_call(
        functools.partial(_matmul_res_kernel, nparts=len(parts)),
        out_shape=jax.ShapeDtypeStruct((m, n), F32),
        grid=(m // tm, n // tn),
        in_specs=in_specs,
        out_specs=pl.BlockSpec((tm, tn), lambda i, j: (i, j)),
        compiler_params=_cparams(("parallel", "arbitrary")),
        name="matmul_residual",
    )(*parts, w, res)


def _chunk_rows(ref, c, rows, nchunk):
    return ref[pl.ds(c, rows, stride=nchunk), :]


def _swiglu_partial(xn_ref, wg_ref, wu_ref, wd_ref):
    xn = xn_ref[...]
    gate = jnp.dot(xn, wg_ref[0], preferred_element_type=F32)
    up = jnp.dot(xn, wu_ref[0], preferred_element_type=F32)
    act = (gate * _sigmoid(gate) * up).astype(BF16)
    return jnp.dot(act, wd_ref[0], preferred_element_type=F32)


def _ffn_kernel(x_ref, g_ref, wg_ref, wu_ref, wd_ref, o_ref, xn_ref):
    j = pl.program_id(1)

    @pl.when(j == 0)
    def _():
        xn_ref[...] = _rms(x_ref[...], g_ref[...]).astype(BF16)

    part = _swiglu_partial(xn_ref, wg_ref, wu_ref, wd_ref)

    @pl.when(j == 0)
    def _():
        o_ref[...] = part + x_ref[...]

    @pl.when(j > 0)
    def _():
        o_ref[...] += part


def ffn_residual(x, g, wg, wu, wd):
    m, d = x.shape
    f = wg.shape[2]
    tm = _pick_tile(m, (ROW_TILE_BIG, TIME_TILE))
    tf = COL_TILE
    return pl.pallas_call(
        _ffn_kernel,
        out_shape=jax.ShapeDtypeStruct((m, d), F32),
        grid=(m // tm, f // tf),
        in_specs=[pl.BlockSpec((tm, d), lambda i, j: (i, 0)),
                  pl.BlockSpec((1, d), lambda i, j: (0, 0)),
                  pl.BlockSpec((1, d, tf), lambda i, j: (0, 0, j)),
                  pl.BlockSpec((1, d, tf), lambda i, j: (0, 0, j)),
                  pl.BlockSpec((1, tf, d), lambda i, j: (0, j, 0))],
        out_specs=pl.BlockSpec((tm, d), lambda i, j: (i, 0)),
        scratch_shapes=[pltpu.VMEM((tm, d), BF16)],
        compiler_params=_cparams(("parallel", "arbitrary")),
        name="ffn_residual",
    )(x, g, wg, wu, wd)


def _experts_kernel(eid_ref, nused_ref, x_ref, g_ref, wg_ref, wu_ref, wd_ref, o_ref,
                    xn_ref, acc_ref):
    i = pl.program_id(0)
    j = pl.program_id(1)
    tm, d = xn_ref.shape
    nchunk = d // LANES

    @pl.when(i < nused_ref[0])
    def _():
        @pl.when(j == 0)
        def _():
            ssq = jnp.zeros((tm, LANES), F32)
            for c in range(nchunk):
                xc = _chunk_rows(x_ref, c, tm, nchunk)
                ssq = ssq + xc * xc
            inv = lax.rsqrt(jnp.sum(ssq, axis=1, keepdims=True) / d + EPS)
            for c in range(nchunk):
                sl = slice(c * LANES, (c + 1) * LANES)
                xn_ref[:, sl] = (_chunk_rows(x_ref, c, tm, nchunk) * inv * g_ref[:, sl]).astype(BF16)

        part = _swiglu_partial(xn_ref, wg_ref, wu_ref, wd_ref)

        @pl.when(j == 0)
        def _():
            acc_ref[...] = part

        @pl.when(j > 0)
        def _():
            acc_ref[...] += part

    @pl.when(j == pl.num_programs(1) - 1)
    def _():
        @pl.when(i >= nused_ref[0])
        def _():
            acc_ref[...] = jnp.zeros_like(acc_ref)

        for c in range(nchunk):
            o_ref[pl.ds(c, tm, stride=nchunk), :] = acc_ref[:, c * LANES:(c + 1) * LANES]


def ffn_experts(x, g, wg, wu, wd, eid, nused, tm):
    d = wg.shape[1]
    f = wg.shape[2]
    nchunk = d // LANES
    nblk = x.shape[0] // (tm * nchunk)
    tf = COL_TILE
    nj = f // tf

    def jj(i, j, nused):
        return jnp.where(i < nused[0], j, nj - 1)

    grid_spec = pltpu.PrefetchScalarGridSpec(
        num_scalar_prefetch=2,
        grid=(nblk, nj),
        in_specs=[pl.BlockSpec((tm * nchunk, LANES), lambda i, j, eid, nu: (i, 0)),
                  pl.BlockSpec((1, d), lambda i, j, eid, nu: (0, 0)),
                  pl.BlockSpec((1, d, tf), lambda i, j, eid, nu: (eid[i], 0, jj(i, j, nu))),
                  pl.BlockSpec((1, d, tf), lambda i, j, eid, nu: (eid[i], 0, jj(i, j, nu))),
                  pl.BlockSpec((1, tf, d), lambda i, j, eid, nu: (eid[i], jj(i, j, nu), 0))],
        out_specs=pl.BlockSpec((tm * nchunk, LANES), lambda i, j, eid, nu: (i, 0)),
        scratch_shapes=[pltpu.VMEM((tm, d), BF16), pltpu.VMEM((tm, d), F32)],
    )
    return pl.pallas_call(
        _experts_kernel,
        out_shape=jax.ShapeDtypeStruct(x.shape, F32),
        grid_spec=grid_spec,
        compiler_params=_cparams(("parallel", "arbitrary")),
        name="ffn_experts",
    )(eid, nused, x, g, wg, wu, wd)


def _lru_kernel(xb_ref, yb_ref, cw_ref, cb_ref, wr_ref, br_ref, wi_ref, bi_ref, lam_ref,
                o_ref, xext_ref, h_ref):
    tb, w = xb_ref.shape

    @pl.when(pl.program_id(1) == 0)
    def _():
        xext_ref[0:SUBLANES, :] = jnp.zeros((SUBLANES, w), F32)
        h_ref[...] = jnp.zeros_like(h_ref)

    xext_ref[SUBLANES:SUBLANES + tb, :] = xb_ref[...].astype(F32)
    cw = cw_ref[...]
    xc = cb_ref[...] + cw[CONV_W - 1:CONV_W, :] * xext_ref[SUBLANES:SUBLANES + tb, :]
    for j in range(CONV_W - 1):
        xc = xc + cw[j:j + 1, :] * xext_ref[pl.ds(SUBLANES - (CONV_W - 1) + j, tb), :]
    xext_ref[0:SUBLANES, :] = xext_ref[tb:tb + SUBLANES, :]

    xcb = xc.astype(BF16)
    nb = wr_ref.shape[0]
    bw = w // nb
    r_parts, i_parts = [], []
    for n in range(nb):
        xs = xcb[:, n * bw:(n + 1) * bw]
        r_parts.append(jnp.dot(xs, wr_ref[n], preferred_element_type=F32))
        i_parts.append(jnp.dot(xs, wi_ref[n], preferred_element_type=F32))
    r = _sigmoid(jnp.concatenate(r_parts, axis=1) + br_ref[...])
    ig = _sigmoid(jnp.concatenate(i_parts, axis=1) + bi_ref[...])

    neg_lam = -lam_ref[...]
    softplus = jnp.maximum(neg_lam, 0.0) + jnp.log1p(jnp.exp(-jnp.abs(neg_lam)))
    log_a = -LRU_C * r * softplus
    a = jnp.exp(log_a)
    u = jnp.sqrt(1.0 - a * a) * (ig * xc)

    row = lax.broadcasted_iota(I32, (tb, w), 0)
    s = 1
    while s < tb:
        keep = row >= s
        a_sh = pltpu.roll(a, s, 0)
        u_sh = pltpu.roll(u, s, 0)
        u = jnp.where(keep, a * u_sh + u, u)
        a = jnp.where(keep, a * a_sh, a)
        s *= 2
    h = a * h_ref[0:1, :] + u
    h_ref[0:1, :] = h[tb - 1:tb, :]

    y = yb_ref[...].astype(F32)
    gelu = 0.5 * y * (1.0 + jnp.tanh(0.7978845608028654 * (y + 0.044715 * (y * y * y))))
    o_ref[...] = (h * gelu).astype(o_ref.dtype)


def rg_lru(z, conv_w, conv_b, w_r, b_r, w_i, b_i, lam):
    b, tp, _ = z.shape
    w = conv_w.shape[1]
    tb = TIME_TILE
    nb = w_r.shape[0]
    bw = w // nb
    full = lambda shape: pl.BlockSpec(shape, lambda bi, ti: (0,) * len(shape))
    return pl.pallas_call(
        _lru_kernel,
        out_shape=jax.ShapeDtypeStruct((b, tp, w), BF16),
        grid=(b, tp // tb),
        in_specs=[pl.BlockSpec((None, tb, w), lambda bi, ti: (bi, ti, 0)),
                  pl.BlockSpec((None, tb, w), lambda bi, ti: (bi, ti, 1)),
                  full((CONV_W, w)), full((1, w)),
                  full((nb, bw, bw)), full((1, w)),
                  full((nb, bw, bw)), full((1, w)),
                  full((1, w))],
        out_specs=pl.BlockSpec((None, tb, w), lambda bi, ti: (bi, ti, 0)),
        scratch_shapes=[pltpu.VMEM((tb + 2 * SUBLANES, w), F32), pltpu.VMEM((SUBLANES, w), F32)],
        compiler_params=_cparams(("parallel", "arbitrary")),
        name="rg_lru",
    )(z, z, conv_w, conv_b, w_r, b_r, w_i, b_i, lam)


def _gla_kernel(q_ref, k_ref, v_ref, g_ref, glr_ref, wgk_ref, bgk_ref, ng_ref, o_ref, st_ref,
                *, chunk, heads):
    tb, hdk = q_ref.shape
    dk = hdk // heads
    dv = v_ref.shape[1] // heads
    c = chunk

    @pl.when(pl.program_id(1) == 0)
    def _():
        st_ref[...] = jnp.zeros_like(st_ref)

    row = lax.broadcasted_iota(I32, (c, hdk), 0)
    tri = lax.broadcasted_iota(I32, (c, c), 0) >= lax.broadcasted_iota(I32, (c, c), 1)
    nt = (((1,), (1,)), ((), ()))
    tn = (((0,), (0,)), ((), ()))

    def body(ci, carry):
        sl = pl.ds(pl.multiple_of(ci * c, c), c)
        pre = jnp.dot(glr_ref[sl, :], wgk_ref[...], preferred_element_type=F32) + bgk_ref[...]
        gk = (jnp.minimum(pre, 0.0) - jnp.log1p(jnp.exp(-jnp.abs(pre)))) * (1.0 / GLA_GATE_NORM)
        bc = gk
        s = 1
        while s < c:
            bc = bc + jnp.where(row >= s, pltpu.roll(bc, s, 0), 0.0)
            s *= 2
        q = q_ref[sl, :].astype(F32) * (dk ** -0.5)
        k = k_ref[sl, :].astype(F32)
        v = v_ref[sl, :]
        g = g_ref[sl, :].astype(F32)
        bmid = bc[c // 2:c // 2 + 1, :]
        blast = bc[c - 1:c, :]
        q_inter = (q * jnp.exp(bc)).astype(BF16)
        q_intra = (q * jnp.exp(bc - bmid)).astype(BF16)
        k_intra = (k * jnp.exp(bmid - bc)).astype(BF16)
        k_state = (k * jnp.exp(blast - bc)).astype(BF16)
        decay = jnp.exp(blast)
        outs = []
        for h in range(heads):
            ks = slice(h * dk, (h + 1) * dk)
            vs = slice(h * dv, (h + 1) * dv)
            st = st_ref[h]
            vh = v[:, vs]
            inter = lax.dot_general(q_inter[:, ks], st.astype(BF16), nt, preferred_element_type=F32)
            att = lax.dot_general(q_intra[:, ks], k_intra[:, ks], nt, preferred_element_type=F32)
            att = jnp.where(tri, att, 0.0).astype(BF16)
            o = inter + jnp.dot(att, vh, preferred_element_type=F32)
            st_ref[h] = st * decay[:, ks] + lax.dot_general(vh, k_state[:, ks], tn,
                                                           preferred_element_type=F32)
            o = _rms(o, ng_ref[...])
            gh = g[:, vs]
            outs.append(o * (gh * _sigmoid(gh)))
        o_ref[sl, :] = jnp.concatenate(outs, axis=1).astype(o_ref.dtype)
        return carry

    lax.fori_loop(0, tb // c, body, 0)


def gla(z, col0, w_gk, b_gk, norm_g, hdk, hdv):
    b, tp, _ = z.shape
    tb = TIME_TILE
    cq, ck = col0 // hdk, col0 // hdk + 1
    cv = (col0 + 2 * hdk) // hdv
    cg = cv + 1
    cl = (col0 + 2 * hdk + 2 * hdv) // LANES
    full = lambda shape: pl.BlockSpec(shape, lambda bi, ti: (0,) * len(shape))
    return pl.pallas_call(
        functools.partial(_gla_kernel, chunk=GLA_CHUNK, heads=GLA_HEADS),
        out_shape=jax.ShapeDtypeStruct((b, tp, hdv), BF16),
        grid=(b, tp // tb),
        in_specs=[pl.BlockSpec((None, tb, hdk), lambda bi, ti: (bi, ti, cq)),
                  pl.BlockSpec((None, tb, hdk), lambda bi, ti: (bi, ti, ck)),
                  pl.BlockSpec((None, tb, hdv), lambda bi, ti: (bi, ti, cv)),
                  pl.BlockSpec((None, tb, hdv), lambda bi, ti: (bi, ti, cg)),
                  pl.BlockSpec((None, tb, LANES), lambda bi, ti: (bi, ti, cl)),
                  full((LANES, hdk)), full((1, hdk)), full((1, hdv // GLA_HEADS))],
        out_specs=pl.BlockSpec((None, tb, hdv), lambda bi, ti: (bi, ti, 0)),
        scratch_shapes=[pltpu.VMEM((GLA_HEADS, hdv // GLA_HEADS, hdk // GLA_HEADS), F32)],
        compiler_params=_cparams(("parallel", "arbitrary")),
        name="gla",
    )(z, z, z, z, z, w_gk, b_gk, norm_g)


def _rope_tile(x, cos, sin_up, sin_dn, half):
    return x * cos + pltpu.roll(x, half, 1) * sin_up + pltpu.roll(x, LANES - half, 1) * sin_dn


def _rope_kernel(qk_ref, qi_ref, kw_ref, tab_ref, qk_out, qi_out, ki_out, wi_out):
    def tabs(kind):
        return tab_ref[3 * kind], tab_ref[3 * kind + 1], tab_ref[3 * kind + 2]

    n_q_tiles = ATT_HEADS
    for t in range(qk_ref.shape[1] // LANES):
        ca, sa_up, sa_dn = tabs(0 if t < n_q_tiles else 1)
        sl = slice(t * LANES, (t + 1) * LANES)
        qk_out[:, sl] = _rope_tile(qk_ref[:, sl].astype(F32), ca, sa_up, sa_dn, 16).astype(BF16)
    cb, sb_up, sb_dn = tabs(2)
    for t in range(qi_ref.shape[1] // LANES):
        sl = slice(t * LANES, (t + 1) * LANES)
        qi_out[:, sl] = _rope_tile(qi_ref[:, sl].astype(F32), cb, sb_up, sb_dn, 8).astype(BF16)
    ck, sk_up, sk_dn = tabs(3)
    kw = _rope_tile(kw_ref[...].astype(F32), ck, sk_up, sk_dn, 8)
    ki_out[...] = kw.astype(BF16)
    wi_out[...] = pltpu.roll(kw, LANES - IDX_DIM, 1)


def rope_tables(tp, head_dim):
    pos = jnp.arange(tp, dtype=F32)[:, None]
    lane = jnp.arange(LANES)

    def tables(dim, period, active):
        half = (dim // 4) // 2
        freqs = ROPE_THETA ** (-jnp.arange(half, dtype=F32) / half)
        ang = pos * freqs[None, :]
        cos, sin = jnp.cos(ang), jnp.sin(ang)
        within = lane % period
        fidx = within % half
        cos_l, sin_l = cos[:, fidx], sin[:, fidx]
        lo = (within < half) & active
        hi = (within >= half) & (within < 2 * half) & active
        c = jnp.where((lo | hi)[None, :], cos_l, 1.0)
        s_up = jnp.where(hi[None, :], sin_l, 0.0)
        s_dn = jnp.where(lo[None, :], -sin_l, 0.0)
        return c, s_up, s_dn

    all_on = jnp.ones((LANES,), bool)
    a = tables(head_dim, head_dim, all_on)
    bq = tables(IDX_DIM, IDX_DIM, all_on)
    ck, sk_up, sk_dn = tables(IDX_DIM, IDX_DIM, lane < IDX_DIM)
    wscale = IDX_HEADS ** -0.5 * IDX_DIM ** -0.5
    ck = jnp.where((lane >= IDX_DIM)[None, :],
                   jnp.where(lane < IDX_DIM + IDX_HEADS, wscale, 0.0)[None, :], ck)
    a_scaled = [t * (head_dim ** -0.5 * LOG2_E) for t in a]
    return jnp.stack([*a_scaled, *a, *bq, ck, sk_up, sk_dn])


def rope(z, tabs, tp, nqk, nqi, col_qi, col_kw):
    m = z.shape[0]
    tm = TIME_TILE
    nper = tp // tm
    return pl.pallas_call(
        _rope_kernel,
        out_shape=[jax.ShapeDtypeStruct((m, nqk), BF16), jax.ShapeDtypeStruct((m, nqi), BF16),
                   jax.ShapeDtypeStruct((m, LANES), BF16), jax.ShapeDtypeStruct((m, LANES), F32)],
        grid=(m // tm,),
        in_specs=[pl.BlockSpec((tm, nqk), lambda i: (i, 0)),
                  pl.BlockSpec((tm, nqi), lambda i: (i, col_qi // nqi)),
                  pl.BlockSpec((tm, LANES), lambda i: (i, col_kw // LANES)),
                  pl.BlockSpec((12, tm, LANES), lambda i: (0, i % nper, 0))],
        out_specs=[pl.BlockSpec((tm, nqk), lambda i: (i, 0)),
                   pl.BlockSpec((tm, nqi), lambda i: (i, 0)),
                   pl.BlockSpec((tm, LANES), lambda i: (i, 0)),
                   pl.BlockSpec((tm, LANES), lambda i: (i, 0))],
        compiler_params=_cparams(("parallel",)),
        name="rope",
    )(z, z, z, tabs)


def _index_keys(qi_stack, w_rows, ki_blk, q0, k0):
    nt = (((1,), (1,)), ((), ()))
    tq = w_rows[0].shape[1]
    lg = lax.dot_general(ki_blk, qi_stack, nt, preferred_element_type=F32)
    score = None
    for h, wh in enumerate(w_rows):
        term = wh * jnp.maximum(lg[:, h * tq:(h + 1) * tq], 0.0)
        score = term if score is None else score + term
    tk = score.shape[0]
    kpos = k0 + lax.broadcasted_iota(I32, (tk, tq), 0)
    qpos = q0 + lax.broadcasted_iota(I32, (tk, tq), 1)
    score = jnp.where(kpos <= qpos, jnp.where(kpos < N_META_TOK, BIG, score), NEG)
    bits = pltpu.bitcast(score, I32)
    keys = jnp.where(bits < 0, bits ^ jnp.int32(0x7FFFFFFF), bits)
    digits = [(keys >> 24) + 128, (keys >> 16) & 255, (keys >> 8) & 255, keys & 255]
    return keys, [d.astype(F32).astype(BF16) for d in digits]


def _float_key(x):
    bits = int(np.float32(x).view(np.int32))
    return bits ^ 0x7FFFFFFF if bits < 0 else bits


def _indexer_kernel(qi_ref, wi_ref, ki_ref, bias_ref, keys_ref, d3_ref, d2_ref, d1_ref, d0_ref,
                    e_ref, qs_ref, cut_ref, *, k_sel, tk):
    tq = qi_ref.shape[0]
    qb = pl.program_id(1)
    q0 = qb * tq
    nkv = (q0 + tq + tk - 1) // tk
    for h in range(IDX_HEADS):
        qs_ref[h * tq:(h + 1) * tq, :] = qi_ref[:, h * IDX_DIM:(h + 1) * IDX_DIM]
    w_t = jnp.transpose(wi_ref[...])
    w_rows = [w_t[h:h + 1, :] for h in range(IDX_HEADS)]

    digit_refs = (d3_ref, d2_ref, d1_ref, d0_ref)

    def rows_of(kb):
        return pl.ds(pl.multiple_of(kb * tk, tk), tk)

    def fill(kb, carry):
        k0 = pl.multiple_of(kb * tk, tk)
        keys, digits = _index_keys(qs_ref[...], w_rows, ki_ref[rows_of(kb), 0:IDX_DIM], q0, k0)
        keys_ref[rows_of(kb), :] = keys
        for ref, dig in zip(digit_refs, digits):
            ref[rows_of(kb), :] = dig
        return carry

    lax.fori_loop(0, nkv, fill, 0)

    npair = (nkv + 1) // 2
    minus1 = -jnp.ones((), BF16)
    for ref in digit_refs:
        ref[rows_of(nkv), :] = jnp.full((tk, tq), minus1, BF16)
    npack = 2 * tk // BF16_ROWS

    def pair_rows(pi):
        return pl.ds(pl.multiple_of(pi * (2 * tk), 2 * tk), 2 * tk)

    def packed(x):
        return jnp.broadcast_to(x, (BF16_ROWS, tq)).astype(BF16)

    def count16(ref, pred):
        one, zero = jnp.ones((), BF16), jnp.zeros((), BF16)

        def blk(pi, acc):
            hit = jnp.where(pred(ref[pair_rows(pi), :].reshape(npack, BF16_ROWS, tq)), one, zero)
            parts = [hit[i] for i in range(npack)]
            while len(parts) > 1:
                parts = [a + b for a, b in zip(parts[0::2], parts[1::2])] + parts[len(parts) & ~1:]
            return acc + parts[0].astype(F32)
        acc = lax.fori_loop(0, npair, blk, jnp.zeros((BF16_ROWS, tq), F32))
        return jnp.sum(acc, axis=0, keepdims=True)

    def narrow(digit_ref, src_ref, match):
        def blk(pi, carry):
            src = src_ref[pair_rows(pi), :].reshape(npack, BF16_ROWS, tq)
            dig = digit_ref[pair_rows(pi), :].reshape(npack, BF16_ROWS, tq)
            e_ref[pair_rows(pi), :] = jnp.where(src == match, dig, minus1).reshape(2 * tk, tq)
            return carry
        lax.fori_loop(0, npair, blk, 0)

    def digit_search(ref, need):
        def step(bi, c):
            cand = c + jnp.left_shift(jnp.int32(1), 7 - bi).astype(F32)
            cb = packed(cand)[None]
            return jnp.where(count16(ref, lambda x: x >= cb) >= need, cand, c)
        return lax.fori_loop(0, 8, step, jnp.zeros((1, tq), F32))

    need = jnp.full((1, tq), k_sel, F32)
    src_ref = d3_ref
    thr_digits = []
    for level in range(4):
        dig = digit_search(src_ref, need)
        thr_digits.append(dig.astype(I32))
        if level < 3:
            dig_b = packed(dig)[None]
            need = need - count16(src_ref, lambda x: x > dig_b)
            narrow(digit_refs[level + 1], src_ref, dig_b)
            src_ref = e_ref
    thr = (((thr_digits[0] - 128) << 24) | (thr_digits[1] << 16) | (thr_digits[2] << 8)
           | thr_digits[3])

    row = lax.broadcasted_iota(I32, (tk, tq), 0)

    def block(kb):
        k0 = pl.multiple_of(kb * tk, tk)
        return keys_ref[pl.ds(k0, tk), :], k0 + row

    def count(pred):
        def blk(kb, acc):
            hit = jnp.where(pred(*block(kb)), 1, 0)
            return acc + jnp.sum(hit.reshape(tk // SUBLANES, SUBLANES, tq), axis=0)
        acc = lax.fori_loop(0, nkv, blk, jnp.zeros((SUBLANES, tq), I32))
        return jnp.sum(acc, axis=0, keepdims=True)

    n_ge = count(lambda keys, kpos: keys >= thr)
    no_tie_cut = jnp.full((1, tq), 1 << 30, I32)
    cut_ref[...] = no_tie_cut

    @pl.when(jnp.max(n_ge) > k_sel)
    def _():
        n_gt = count(lambda keys, kpos: keys > thr)
        need = k_sel - n_gt

        def cut_step(bi, cut):
            cand = cut + jnp.left_shift(jnp.int32(1), 15 - bi)
            n = count(lambda keys, kpos: (keys == thr) & (kpos < cand))
            return jnp.where(n <= need, cand, cut)

        cut = lax.fori_loop(0, 16, cut_step, jnp.zeros((1, tq), I32))
        cut_ref[...] = jnp.where(n_ge > k_sel, cut, no_tie_cut)

    bias_ref[...] = jnp.full(bias_ref.shape, NEG, BF16)
    cut = cut_ref[...]
    key_valid = jnp.int32(_float_key(0.5 * NEG))

    def emit(kb, carry):
        k0 = pl.multiple_of(kb * tk, tk)
        keys, kpos = block(kb)
        chosen = ((keys > thr) | ((keys == thr) & (kpos < cut))) & (keys > key_valid)
        bias_ref[pl.ds(k0, tk), :] = jnp.where(chosen, 0.0, NEG).astype(BF16)
        return carry

    lax.fori_loop(0, nkv, emit, 0)


def indexer(qi, wi, ki, b, tp, k_sel):
    tq = INDEX_Q_TILE
    nq = tp // tq
    return pl.pallas_call(
        functools.partial(_indexer_kernel, k_sel=k_sel, tk=TIME_TILE),
        out_shape=jax.ShapeDtypeStruct((b * tp, tp), BF16),
        grid=(b, nq),
        in_specs=[pl.BlockSpec((tq, qi.shape[1]), lambda bi, qb: (bi * nq + qb, 0)),
                  pl.BlockSpec((tq, LANES), lambda bi, qb: (bi * nq + qb, 0)),
                  pl.BlockSpec((tp, LANES), lambda bi, qb: (bi, 0))],
        out_specs=pl.BlockSpec((tp, tq), lambda bi, qb: (bi, qb)),
        scratch_shapes=[pltpu.VMEM((tp, tq), I32)] + [pltpu.VMEM((tp + TIME_TILE, tq), BF16)] * 5 + [
                        pltpu.VMEM((IDX_HEADS * tq, IDX_DIM), BF16), pltpu.VMEM((1, tq), I32)],
        compiler_params=_cparams(("parallel", "arbitrary")),
        name="indexer",
    )(qi, wi, ki)


def _attn_kernel(qb_ref, kb_ref, q_ref, k_ref, v_ref, bias_ref, o_ref, qs_ref, m_ref, l_ref, acc_ref):
    tq = q_ref.shape[0]
    tk = k_ref.shape[0]
    hd = k_ref.shape[1] // ATT_KV_HEADS
    grows = ATT_GROUPS * tq
    qb = qb_ref[pl.program_id(1)]
    kb = kb_ref[pl.program_id(1)]
    nt = (((1,), (1,)), ((), ()))

    @pl.when(kb == 0)
    def _():
        m_ref[...] = jnp.full_like(m_ref, NEG)
        l_ref[...] = jnp.zeros_like(l_ref)
        acc_ref[...] = jnp.zeros_like(acc_ref)
        for hi in range(ATT_HEADS):
            qs_ref[hi * tq:(hi + 1) * tq, :] = q_ref[:, hi * hd:(hi + 1) * hd]

    bias = jnp.transpose(bias_ref[...].astype(F32))[None]
    ones = jnp.ones((tk, LANES), BF16)
    for kvh in range(ATT_KV_HEADS):
        rows = slice(kvh * grows, (kvh + 1) * grows)
        kh = k_ref[:, kvh * hd:(kvh + 1) * hd]
        vh1 = jnp.concatenate([v_ref[:, kvh * hd:(kvh + 1) * hd], ones], axis=1)
        s = lax.dot_general(qs_ref[rows, :], kh, nt, preferred_element_type=F32)
        s = (s.reshape(ATT_GROUPS, tq, tk) + bias).reshape(grows, tk)
        m_old = m_ref[rows, :]
        m_new = jnp.maximum(m_old, jnp.broadcast_to(jnp.max(s, axis=1, keepdims=True),
                                                    (grows, LANES)))
        p = jnp.concatenate(
            [jnp.exp2(s[:, t * LANES:(t + 1) * LANES] - m_new).astype(BF16)
             for t in range(tk // LANES)], axis=1)
        alpha = jnp.exp2(m_old - m_new)
        pv = jnp.dot(p, vh1, preferred_element_type=F32)
        l_ref[rows, :] = alpha * l_ref[rows, :] + pv[:, hd:]
        acc_ref[rows, :] = alpha * acc_ref[rows, :] + pv[:, :hd]
        m_ref[rows, :] = m_new

    @pl.when(kb == qb)
    def _():
        for hi in range(ATT_HEADS):
            rows = slice(hi * tq, (hi + 1) * tq)
            o_ref[:, hi * hd:(hi + 1) * hd] = (acc_ref[rows, :] / l_ref[rows, :]).astype(o_ref.dtype)


def sparse_attention(qk, z, bias, b, tp, d, col_v):
    tq = tk = TIME_TILE
    nq = tp // tq
    hd = d // ATT_HEADS
    kvw = ATT_KV_HEADS * hd
    pairs = [(qb, kb) for qb in range(nq) for kb in range(qb + 1)]
    qb_of = jnp.asarray([p[0] for p in pairs], I32)
    kb_of = jnp.asarray([p[1] for p in pairs], I32)
    qrow = lambda bi, i, qbs, kbs: bi * nq + qbs[i]
    kvrow = lambda bi, i, qbs, kbs: bi * nq + kbs[i]
    grid_spec = pltpu.PrefetchScalarGridSpec(
        num_scalar_prefetch=2,
        grid=(b, len(pairs)),
        in_specs=[pl.BlockSpec((tq, d), lambda bi, i, qbs, kbs: (qrow(bi, i, qbs, kbs), 0)),
                  pl.BlockSpec((tk, kvw), lambda bi, i, qbs, kbs: (kvrow(bi, i, qbs, kbs), d // kvw)),
                  pl.BlockSpec((tk, kvw),
                               lambda bi, i, qbs, kbs: (kvrow(bi, i, qbs, kbs), col_v // kvw)),
                  pl.BlockSpec((tk, tq), lambda bi, i, qbs, kbs: (kvrow(bi, i, qbs, kbs), qbs[i]))],
        out_specs=pl.BlockSpec((tq, d), lambda bi, i, qbs, kbs: (qrow(bi, i, qbs, kbs), 0)),
        scratch_shapes=[pltpu.VMEM((ATT_HEADS * tq, hd), BF16),
                        pltpu.VMEM((ATT_HEADS * tq, LANES), F32),
                        pltpu.VMEM((ATT_HEADS * tq, LANES), F32),
                        pltpu.VMEM((ATT_HEADS * tq, hd), F32)],
    )
    return pl.pallas_call(
        _attn_kernel,
        out_shape=jax.ShapeDtypeStruct((b * tp, d), BF16),
        grid_spec=grid_spec,
        compiler_params=_cparams(("parallel", "arbitrary")),
        name="sparse_attention",
    )(qb_of, kb_of, qk, qk, z, bias)


def _router_kernel(x_ref, g_ref, w_ref, o_ref, xt_ref):
    tm, d = x_ref.shape
    nchunk = d // LANES
    for c in range(nchunk):
        xt_ref[pl.ds(c, tm, stride=nchunk), :] = x_ref[:, c * LANES:(c + 1) * LANES]
    xn = _rms(x_ref[...], g_ref[...]).astype(BF16)
    logits = jnp.dot(xn, w_ref[...], preferred_element_type=F32)
    lane = lax.broadcasted_iota(I32, logits.shape, 1)
    logits = jnp.where(lane < N_EXPERTS, logits, -jnp.inf)
    v1 = jnp.max(logits, axis=1, keepdims=True)
    e1 = jnp.min(jnp.where(logits == v1, lane, LANES), axis=1, keepdims=True)
    rest = jnp.where(lane == e1, -jnp.inf, logits)
    v2 = jnp.max(rest, axis=1, keepdims=True)
    e2 = jnp.min(jnp.where(rest == v2, lane, LANES), axis=1, keepdims=True)
    ex = jnp.exp(v2 - v1)
    g1 = 1.0 / (1.0 + ex)
    g2 = ex / (1.0 + ex)
    out = jnp.where(lane == 0, g1, jnp.where(lane == 1, g2, 0.0))
    out = jnp.where(lane == 2, e1.astype(F32), jnp.where(lane == 3, e2.astype(F32), out))
    o_ref[...] = out


def router(x, g, w):
    m, d = x.shape
    tm = TIME_TILE
    nchunk = d // LANES
    return pl.pallas_call(
        _router_kernel,
        out_shape=[jax.ShapeDtypeStruct((m, LANES), F32),
                   jax.ShapeDtypeStruct((m * nchunk, LANES), F32)],
        grid=(m // tm,),
        in_specs=[pl.BlockSpec((tm, d), lambda i: (i, 0)),
                  pl.BlockSpec((1, d), lambda i: (0, 0)),
                  pl.BlockSpec((d, LANES), lambda i: (0, 0))],
        out_specs=[pl.BlockSpec((tm, LANES), lambda i: (i, 0)),
                   pl.BlockSpec((tm * nchunk, LANES), lambda i: (i, 0))],
        compiler_params=_cparams(("parallel",)),
        name="router",
    )(x, g, w)


def _gather_kernel(idx_ref, src_ref, dst_ref, sem, *, rows, nchunk):
    base = pl.program_id(0) * rows

    def token_copy(r):
        src_row = pl.multiple_of(idx_ref[base + r] * nchunk, nchunk)
        dst_row = pl.multiple_of(r * nchunk, nchunk)
        return pltpu.make_async_copy(src_ref.at[pl.ds(src_row, nchunk)],
                                     dst_ref.at[pl.ds(dst_row, nchunk)], sem)

    def start(r, carry):
        token_copy(r).start()
        return carry

    def wait(r, carry):
        token_copy(r).wait()
        return carry

    lax.fori_loop(0, rows, start, 0)
    lax.fori_loop(0, rows, wait, 0)


def gather_tokens(src, idx, nchunk):
    n = idx.shape[0]
    rows = _pick_tile(n, (GATHER_ROWS, TIME_TILE, 256, 128))
    grid_spec = pltpu.PrefetchScalarGridSpec(
        num_scalar_prefetch=1,
        grid=(n // rows,),
        in_specs=[pl.BlockSpec(memory_space=pl.ANY)],
        out_specs=pl.BlockSpec((rows * nchunk, LANES), lambda i, idx: (i, 0)),
        scratch_shapes=[pltpu.SemaphoreType.DMA(())],
    )
    return pl.pallas_call(
        functools.partial(_gather_kernel, rows=rows, nchunk=nchunk),
        out_shape=jax.ShapeDtypeStruct((n * nchunk, LANES), src.dtype),
        grid_spec=grid_spec,
        compiler_params=_cparams(("arbitrary",)),
        name="gather_tokens",
    )(idx, src)


def _combine_norm_kernel(h_ref, y1_ref, y2_ref, route_ref, g_ref, o_ref):
    tm, d = h_ref.shape
    nchunk = d // LANES
    g1 = route_ref[:, 0:1]
    g2 = route_ref[:, 1:2]
    ssq = jnp.zeros((tm, LANES), F32)
    for c in range(nchunk):
        sl = slice(c * LANES, (c + 1) * LANES)
        hc = (h_ref[:, sl] + g1 * _chunk_rows(y1_ref, c, tm, nchunk)
              + g2 * _chunk_rows(y2_ref, c, tm, nchunk))
        o_ref[:, sl] = hc
        ssq = ssq + hc * hc
    inv = lax.rsqrt(jnp.sum(ssq, axis=1, keepdims=True) / d + EPS)
    o_ref[...] = o_ref[...] * inv * g_ref[...]


def combine_norm(h, y, route, g):
    m, d = h.shape
    tm = TIME_TILE
    nblk = m // tm
    nchunk = d // LANES
    return pl.pallas_call(
        _combine_norm_kernel,
        out_shape=jax.ShapeDtypeStruct((m, d), F32),
        grid=(nblk,),
        in_specs=[pl.BlockSpec((tm, d), lambda i: (i, 0)),
                  pl.BlockSpec((tm * nchunk, LANES), lambda i: (i, 0)),
                  pl.BlockSpec((tm * nchunk, LANES), lambda i: (i + nblk, 0)),
                  pl.BlockSpec((tm, LANES), lambda i: (i, 0)),
                  pl.BlockSpec((1, d), lambda i: (0, 0))],
        out_specs=pl.BlockSpec((tm, d), lambda i: (i, 0)),
        compiler_params=_cparams(("parallel",)),
        name="combine_norm",
    )(h, y, y, route, g)


def _pad_cols(w, n):
    return jnp.pad(w, ((0, 0), (0, n - w.shape[1])))


def _round_up(n, k):
    return -(-n // k) * k


def layer_ab(h, b, tp, norm_g, w_in, conv_w, conv_b, w_r, b_r, w_i, b_i, lam, w_gk, b_gk,
             gla_norm, w_out, ffn_norm, ffn_wg, ffn_wu, ffn_wd):
    m, d = h.shape
    lw = conv_w.shape[1]
    hdk = w_gk.shape[1]
    hdv = d - lw
    n_in = _round_up(w_in.shape[1] - GLA_RANK + LANES, COL_TILE)
    z = norm_matmul(h, norm_g[None], _pad_cols(w_in, n_in).astype(BF16)).reshape(b, tp, n_in)
    a_out = rg_lru(z, conv_w, conv_b[None], w_r.astype(BF16), b_r.reshape(1, lw),
                   w_i.astype(BF16), b_i.reshape(1, lw), lam[None])
    w_gk_p = jnp.pad(w_gk, ((0, LANES - GLA_RANK), (0, 0))).astype(BF16)
    b_out = gla(z, 2 * lw, w_gk_p, b_gk[None], gla_norm[None], hdk, hdv)
    h = matmul_residual([a_out.reshape(m, lw), b_out.reshape(m, hdv)], w_out.astype(BF16), h)
    return ffn_residual(h, ffn_norm[None], ffn_wg.astype(BF16)[None], ffn_wu.astype(BF16)[None],
                        ffn_wd.astype(BF16)[None])


def layer_c(h, b, tp, t_real, norm_g, w_in, w_out):
    m, d = h.shape
    hd = d // ATT_HEADS
    kvw = ATT_KV_HEADS * hd
    nqi = IDX_HEADS * IDX_DIM
    col_v = d + kvw
    col_qi = d + 2 * kvw
    col_kw = col_qi + nqi
    n_in = _round_up(col_kw + LANES, COL_TILE)
    z = norm_matmul(h, norm_g[None], _pad_cols(w_in, n_in).astype(BF16))
    qk, qi, ki, wi = rope(z, rope_tables(tp, hd), tp, d + kvw, nqi, col_qi, col_kw)
    k_sel = min(TOPK_MAX, (t_real - N_META_TOK) // 4)
    bias = indexer(qi, wi, ki, b, tp, k_sel)
    att = sparse_attention(qk, z, bias, b, tp, d, col_v)
    return matmul_residual([att], w_out.astype(BF16), h)


def moe_dispatch(route, tm):
    m = route.shape[0]
    e_flat = route[:, 2:4].astype(I32).reshape(-1)
    onehot = (e_flat[:, None] == jnp.arange(N_EXPERTS)[None, :]).astype(I32)
    rank = jnp.take_along_axis(jnp.cumsum(onehot, axis=0) - onehot, e_flat[:, None], axis=1)[:, 0]
    counts = jnp.sum(onehot, axis=0)
    padded = (counts + tm - 1) // tm * tm
    pad_end = jnp.cumsum(padded)
    dest = (pad_end - padded)[e_flat] + rank
    nblk = -(-(2 * m) // tm) + N_EXPERTS
    slot_tok = jnp.zeros((nblk * tm,), I32).at[dest].set(jnp.arange(2 * m, dtype=I32) // 2)
    nused = (pad_end[-1] // tm).astype(I32)
    blk = jnp.minimum(jnp.arange(nblk, dtype=I32), nused - 1)
    blk_expert = jnp.minimum(jnp.sum((pad_end[None, :] <= (blk * tm)[:, None]).astype(I32), axis=1),
                             N_EXPERTS - 1)
    return slot_tok, dest.reshape(m, 2), blk_expert, nused.reshape(1)


def layer_moe_final(h, norm_g, w_router, wg, wu, wd, final_g):
    m, d = h.shape
    nchunk = d // LANES
    route, h_tok = router(h, norm_g[None], _pad_cols(w_router, LANES).astype(BF16))
    tm = _pick_tile(m, (ROW_TILE_BIG, TIME_TILE))
    slot_tok, tok_slots, blk_expert, nused = moe_dispatch(route, tm)
    x_slots = gather_tokens(h_tok, slot_tok, nchunk)
    y_slots = ffn_experts(x_slots, norm_g[None], wg.astype(BF16), wu.astype(BF16), wd.astype(BF16),
                          blk_expert, nused, tm)
    y_tok = gather_tokens(y_slots, tok_slots.T.reshape(-1), nchunk)
    return combine_norm(h, y_tok, route, final_g[None])


def kernel(x, meta, ab_norm, ab_w_in, lru_conv_w, lru_conv_b, lru_w_r, lru_b_r, lru_w_i, lru_b_i,
           lru_lam, gla_w_gk, gla_b_gk, gla_norm, ab_w_out, ffn_norm, ffn_w_gate, ffn_w_up,
           ffn_w_down, c_norm, c_w_in, c_w_out, moe_norm, moe_router, moe_w_gate, moe_w_up,
           moe_w_down, final_norm):
    b, seq, d = x.shape
    assert ab_norm.shape[0] == 1 and c_norm.shape[0] == 1, "two-layer trunk only"
    t_real = seq + N_META_TOK
    tp = _round_up(t_real, TIME_TILE)
    h = jnp.concatenate([jnp.broadcast_to(meta.astype(x.dtype)[None], (b, N_META_TOK, d)), x,
                         jnp.zeros((b, tp - t_real, d), x.dtype)], axis=1).reshape(b * tp, d)
    h = layer_ab(h, b, tp, ab_norm[0], ab_w_in[0], lru_conv_w[0], lru_conv_b[0], lru_w_r[0],
                 lru_b_r[0], lru_w_i[0], lru_b_i[0], lru_lam[0], gla_w_gk[0], gla_b_gk[0],
                 gla_norm[0], ab_w_out[0], ffn_norm[0], ffn_w_gate[0], ffn_w_up[0], ffn_w_down[0])
    h = layer_c(h, b, tp, t_real, c_norm[0], c_w_in[0], c_w_out[0])
    out = layer_moe_final(h, moe_norm[0], moe_router[0], moe_w_gate[0], moe_w_up[0],
                          moe_w_down[0], final_norm)
    return out.reshape(b, tp, d)[:, N_META_TOK:t_real]
```

```python
import functools

import jax
import jax.numpy as jnp
import numpy as np
from jax import lax
from jax.experimental import pallas as pl
from jax.experimental.pallas import tpu as pltpu

F32 = jnp.float32
BF16 = jnp.bfloat16
I32 = jnp.int32

N_META_TOK = 16
EPS = 1e-6
ROPE_THETA = 500000.0
LRU_BLOCKS = 4
CONV_W = 4
LRU_C = 8.0
GLA_HEADS = 4
GLA_RANK = 16
GLA_GATE_NORM = 16.0
ATT_HEADS = 16
ATT_KV_HEADS = 4
ATT_GROUPS = ATT_HEADS // ATT_KV_HEADS
IDX_HEADS = 8
IDX_DIM = 64
TOPK_MAX = 256
BIG = 1e30
NEG = -1e30
N_EXPERTS = 8
LOG2_E = 1.4426950408889634

LANES = 128
SUBLANES = 8
TIME_TILE = 384
INDEX_Q_TILE = 128
ROW_TILE_BIG = 768
COL_TILE = 512
GLA_CHUNK = 64
VMEM_LIMIT = 56 * 1024 * 1024
INT_MIN = -(2 ** 31)


def _cparams(sem):
    return pltpu.CompilerParams(dimension_semantics=sem, vmem_limit_bytes=VMEM_LIMIT)


def _rms(x, g):
    return x * lax.rsqrt(jnp.mean(x * x, axis=-1, keepdims=True) + EPS) * g


def _sigmoid(x):
    return 1.0 / (1.0 + jnp.exp(-x))


def _pick_tile(n, candidates):
    for c in candidates:
        if n % c == 0:
            return c
    raise ValueError(f"no tile in {candidates} divides {n}")


def _norm_matmul_kernel(x_ref, g_ref, w_ref, o_ref, xn_ref):
    @pl.when(pl.program_id(1) == 0)
    def _():
        xn_ref[...] = _rms(x_ref[...], g_ref[...]).astype(BF16)

    o_ref[...] = jnp.dot(xn_ref[...], w_ref[...], preferred_element_type=F32).astype(o_ref.dtype)


def norm_matmul(x, g, w):
    m, d = x.shape
    n = w.shape[1]
    tm = _pick_tile(m, (ROW_TILE_BIG, TIME_TILE))
    tn = COL_TILE
    return pl.pallas_call(
        _norm_matmul_kernel,
        out_shape=jax.ShapeDtypeStruct((m, n), BF16),
        grid=(m // tm, n // tn),
        in_specs=[pl.BlockSpec((tm, d), lambda i, j: (i, 0)),
                  pl.BlockSpec((1, d), lambda i, j: (0, 0)),
                  pl.BlockSpec((d, tn), lambda i, j: (0, j))],
        out_specs=pl.BlockSpec((tm, tn), lambda i, j: (i, j)),
        scratch_shapes=[pltpu.VMEM((tm, d), BF16)],
        compiler_params=_cparams(("parallel", "arbitrary")),
        name="norm_matmul",
    )(x, g, w)


def _matmul_res_kernel(*refs, nparts):
    xs = refs[:nparts]
    w_ref, r_ref, o_ref = refs[nparts:]
    acc = r_ref[...]
    k0 = 0
    for x_ref in xs:
        kp = x_ref.shape[1]
        acc = acc + jnp.dot(x_ref[...], w_ref[k0:k0 + kp, :], preferred_element_type=F32)
        k0 += kp
    o_ref[...] = acc


def matmul_residual(parts, w, res):
    m, n = res.shape
    k = w.shape[0]
    tm = _pick_tile(m, (ROW_TILE_BIG, TIME_TILE))
    tn = 1024
    in_specs = [pl.BlockSpec((tm, p.shape[1]), lambda i, j: (i, 0)) for p in parts]
    in_specs += [pl.BlockSpec((k, tn), lambda i, j: (0, j)),
                 pl.BlockSpec((tm, tn), lambda i, j: (i, j))]
    return pl.pallas_call(
        functools.partial(_matmul_res_kernel, nparts=len(parts)),
        out_shape=jax.ShapeDtypeStruct((m, n), F32),
        grid=(m // tm, n // tn),
        in_specs=in_specs,
        out_specs=pl.BlockSpec((tm, tn), lambda i, j: (i, j)),
        compiler_params=_cparams(("parallel", "arbitrary")),
        name="matmul_residual",
    )(*parts, w, res)


def _chunk_rows(ref, c, rows, nchunk):
    return ref[pl.ds(c, rows, stride=nchunk), :]


def _swiglu_partial(xn_ref, wg_ref, wu_ref, wd_ref):
    xn = xn_ref[...]
    gate = jnp.dot(xn, wg_ref[0], preferred_element_type=F32)
    up = jnp.dot(xn, wu_ref[0], preferred_element_type=F32)
    act = (gate * _sigmoid(gate) * up).astype(BF16)
    return jnp.dot(act, wd_ref[0], preferred_element_type=F32)


def _ffn_kernel(x_ref, g_ref, wg_ref, wu_ref, wd_ref, o_ref, xn_ref):
    j = pl.program_id(1)

    @pl.when(j == 0)
    def _():
        xn_ref[...] = _rms(x_ref[...], g_ref[...]).astype(BF16)

    part = _swiglu_partial(xn_ref, wg_ref, wu_ref, wd_ref)

    @pl.when(j == 0)
    def _():
        o_ref[...] = part + x_ref[...]

    @pl.when(j > 0)
    def _():
        o_ref[...] += part


def ffn_residual(x, g, wg, wu, wd):
    m, d = x.shape
    f = wg.shape[2]
    tm = _pick_tile(m, (ROW_TILE_BIG, TIME_TILE))
    tf = COL_TILE
    return pl.pallas_call(
        _ffn_kernel,
        out_shape=jax.ShapeDtypeStruct((m, d), F32),
        grid=(m // tm, f // tf),
        in_specs=[pl.BlockSpec((tm, d), lambda i, j: (i, 0)),
                  pl.BlockSpec((1, d), lambda i, j: (0, 0)),
                  pl.BlockSpec((1, d, tf), lambda i, j: (0, 0, j)),
                  pl.BlockSpec((1, d, tf), lambda i, j: (0, 0, j)),
                  pl.BlockSpec((1, tf, d), lambda i, j: (0, j, 0))],
        out_specs=pl.BlockSpec((tm, d), lambda i, j: (i, 0)),
        scratch_shapes=[pltpu.VMEM((tm, d), BF16)],
        compiler_params=_cparams(("parallel", "arbitrary")),
        name="ffn_residual",
    )(x, g, wg, wu, wd)


def _token_gather(idx_ref, first, count, src_ref, dst_ref, sem, nchunk):
    def copy(r):
        src_row = pl.multiple_of(idx_ref[first + r] * nchunk, nchunk)
        dst_row = pl.multiple_of(r * nchunk, nchunk)
        return pltpu.make_async_copy(src_ref.at[pl.ds(src_row, nchunk)],
                                     dst_ref.at[pl.ds(dst_row, nchunk)], sem)

    def start():
        lax.fori_loop(0, count, lambda r, c: (copy(r).start(), c)[1], 0)

    def wait():
        lax.fori_loop(0, count, lambda r, c: (copy(r).wait(), c)[1], 0)

    return start, wait


def _experts_kernel(eid_ref, nused_ref, tok_ref, x_hbm, g_ref, wg_ref, wu_ref, wd_ref, o_ref,
                    xbuf_ref, sem_ref, xn_ref, acc_ref):
    i = pl.program_id(0)
    j = pl.program_id(1)
    tm, d = xn_ref.shape
    nchunk = d // LANES
    nused = nused_ref[0]

    def gather(blk):
        slot = blk % 2
        return _token_gather(tok_ref, blk * tm, tm, x_hbm, xbuf_ref.at[slot], sem_ref.at[slot],
                             nchunk)

    @pl.when(i < nused)
    def _():
        @pl.when(j == 0)
        def _():
            @pl.when(i == 0)
            def _():
                gather(i)[0]()

            gather(i)[1]()

            @pl.when(i + 1 < nused)
            def _():
                gather(i + 1)[0]()

            x_ref = xbuf_ref.at[i % 2]
            ssq = jnp.zeros((tm, LANES), F32)
            for c in range(nchunk):
                xc = _chunk_rows(x_ref, c, tm, nchunk)
                ssq = ssq + xc * xc
            inv = lax.rsqrt(jnp.sum(ssq, axis=1, keepdims=True) / d + EPS)
            for c in range(nchunk):
                sl = slice(c * LANES, (c + 1) * LANES)
                xn_ref[:, sl] = (_chunk_rows(x_ref, c, tm, nchunk) * inv * g_ref[:, sl]).astype(BF16)

        part = _swiglu_partial(xn_ref, wg_ref, wu_ref, wd_ref)

        @pl.when(j == 0)
        def _():
            acc_ref[...] = part

        @pl.when(j > 0)
        def _():
            acc_ref[...] += part

    @pl.when(j == pl.num_programs(1) - 1)
    def _():
        @pl.when(i >= nused_ref[0])
        def _():
            acc_ref[...] = jnp.zeros_like(acc_ref)

        for c in range(nchunk):
            o_ref[pl.ds(c, tm, stride=nchunk), :] = acc_ref[:, c * LANES:(c + 1) * LANES]


def ffn_experts(x_tok, slot_tok, g, wg, wu, wd, eid, nused, tm):
    d = wg.shape[1]
    f = wg.shape[2]
    nchunk = d // LANES
    nblk = slot_tok.shape[0] // tm
    tf = COL_TILE
    nj = f // tf

    def jj(i, j, nused):
        return jnp.where(i < nused[0], j, nj - 1)

    grid_spec = pltpu.PrefetchScalarGridSpec(
        num_scalar_prefetch=3,
        grid=(nblk, nj),
        in_specs=[pl.BlockSpec(memory_space=pl.ANY),
                  pl.BlockSpec((1, d), lambda i, j, eid, nu, tok: (0, 0)),
                  pl.BlockSpec((1, d, tf), lambda i, j, eid, nu, tok: (eid[i], 0, jj(i, j, nu))),
                  pl.BlockSpec((1, d, tf), lambda i, j, eid, nu, tok: (eid[i], 0, jj(i, j, nu))),
                  pl.BlockSpec((1, tf, d), lambda i, j, eid, nu, tok: (eid[i], jj(i, j, nu), 0))],
        out_specs=pl.BlockSpec((tm * nchunk, LANES), lambda i, j, eid, nu, tok: (i, 0)),
        scratch_shapes=[pltpu.VMEM((2, tm * nchunk, LANES), F32), pltpu.SemaphoreType.DMA((2,)),
                        pltpu.VMEM((tm, d), BF16), pltpu.VMEM((tm, d), F32)],
    )
    return pl.pallas_call(
        _experts_kernel,
        out_shape=jax.ShapeDtypeStruct((nblk * tm * nchunk, LANES), F32),
        grid_spec=grid_spec,
        compiler_params=_cparams(("arbitrary", "arbitrary")),
        name="ffn_experts",
    )(eid, nused, slot_tok, x_tok, g, wg, wu, wd)


def _lru_kernel(xb_ref, yb_ref, cw_ref, cb_ref, wr_ref, br_ref, wi_ref, bi_ref, lam_ref,
                o_ref, xext_ref, h_ref):
    tb, w = xb_ref.shape

    @pl.when(pl.program_id(1) == 0)
    def _():
        xext_ref[0:SUBLANES, :] = jnp.zeros((SUBLANES, w), F32)
        h_ref[...] = jnp.zeros_like(h_ref)

    xext_ref[SUBLANES:SUBLANES + tb, :] = xb_ref[...].astype(F32)
    cw = cw_ref[...]
    xc = cb_ref[...] + cw[CONV_W - 1:CONV_W, :] * xext_ref[SUBLANES:SUBLANES + tb, :]
    for j in range(CONV_W - 1):
        xc = xc + cw[j:j + 1, :] * xext_ref[pl.ds(SUBLANES - (CONV_W - 1) + j, tb), :]
    xext_ref[0:SUBLANES, :] = xext_ref[tb:tb + SUBLANES, :]

    xcb = xc.astype(BF16)
    nb = wr_ref.shape[0]
    bw = w // nb
    r_parts, i_parts = [], []
    for n in range(nb):
        xs = xcb[:, n * bw:(n + 1) * bw]
        r_parts.append(jnp.dot(xs, wr_ref[n], preferred_element_type=F32))
        i_parts.append(jnp.dot(xs, wi_ref[n], preferred_element_type=F32))
    r = _sigmoid(jnp.concatenate(r_parts, axis=1) + br_ref[...])
    ig = _sigmoid(jnp.concatenate(i_parts, axis=1) + bi_ref[...])

    neg_lam = -lam_ref[...]
    softplus = jnp.maximum(neg_lam, 0.0) + jnp.log1p(jnp.exp(-jnp.abs(neg_lam)))
    log_a = -LRU_C * r * softplus
    a = jnp.exp(log_a)
    u = jnp.sqrt(1.0 - a * a) * (ig * xc)

    row = lax.broadcasted_iota(I32, (tb, w), 0)
    s = 1
    while s < tb:
        keep = row >= s
        a_sh = pltpu.roll(a, s, 0)
        u_sh = pltpu.roll(u, s, 0)
        u = jnp.where(keep, a * u_sh + u, u)
        a = jnp.where(keep, a * a_sh, a)
        s *= 2
    h = a * h_ref[0:1, :] + u
    h_ref[0:1, :] = h[tb - 1:tb, :]

    y = yb_ref[...].astype(F32)
    gelu = 0.5 * y * (1.0 + jnp.tanh(0.7978845608028654 * (y + 0.044715 * (y * y * y))))
    o_ref[...] = (h * gelu).astype(o_ref.dtype)


def rg_lru(z, conv_w, conv_b, w_r, b_r, w_i, b_i, lam):
    b, tp, _ = z.shape
    w = conv_w.shape[1]
    tb = TIME_TILE
    nb = w_r.shape[0]
    bw = w // nb
    full = lambda shape: pl.BlockSpec(shape, lambda bi, ti: (0,) * len(shape))
    return pl.pallas_call(
        _lru_kernel,
        out_shape=jax.ShapeDtypeStruct((b, tp, w), BF16),
        grid=(b, tp // tb),
        in_specs=[pl.BlockSpec((None, tb, w), lambda bi, ti: (bi, ti, 0)),
                  pl.BlockSpec((None, tb, w), lambda bi, ti: (bi, ti, 1)),
                  full((CONV_W, w)), full((1, w)),
                  full((nb, bw, bw)), full((1, w)),
                  full((nb, bw, bw)), full((1, w)),
                  full((1, w))],
        out_specs=pl.BlockSpec((None, tb, w), lambda bi, ti: (bi, ti, 0)),
        scratch_shapes=[pltpu.VMEM((tb + 2 * SUBLANES, w), F32), pltpu.VMEM((SUBLANES, w), F32)],
        compiler_params=_cparams(("parallel", "arbitrary")),
        name="rg_lru",
    )(z, z, conv_w, conv_b, w_r, b_r, w_i, b_i, lam)


def _gla_kernel(q_ref, k_ref, v_ref, g_ref, glr_ref, wgk_ref, bgk_ref, ng_ref, o_ref, st_ref,
                *, chunk, heads):
    tb, hdk = q_ref.shape
    dk = hdk // heads
    dv = v_ref.shape[1] // heads
    c = chunk

    @pl.when(pl.program_id(1) == 0)
    def _():
        st_ref[...] = jnp.zeros_like(st_ref)

    row = lax.broadcasted_iota(I32, (c, hdk), 0)
    tri = lax.broadcasted_iota(I32, (c, c), 0) >= lax.broadcasted_iota(I32, (c, c), 1)
    nt = (((1,), (1,)), ((), ()))
    tn = (((0,), (0,)), ((), ()))

    def body(ci, carry):
        sl = pl.ds(pl.multiple_of(ci * c, c), c)
        pre = jnp.dot(glr_ref[sl, :], wgk_ref[...], preferred_element_type=F32) + bgk_ref[...]
        gk = (jnp.minimum(pre, 0.0) - jnp.log1p(jnp.exp(-jnp.abs(pre)))) * (1.0 / GLA_GATE_NORM)
        bc = gk
        s = 1
        while s < c:
            bc = bc + jnp.where(row >= s, pltpu.roll(bc, s, 0), 0.0)
            s *= 2
        q = q_ref[sl, :].astype(F32) * (dk ** -0.5)
        k = k_ref[sl, :].astype(F32)
        v = v_ref[sl, :]
        g = g_ref[sl, :].astype(F32)
        bmid = bc[c // 2:c // 2 + 1, :]
        blast = bc[c - 1:c, :]
        q_inter = (q * jnp.exp(bc)).astype(BF16)
        q_intra = (q * jnp.exp(bc - bmid)).astype(BF16)
        k_intra = (k * jnp.exp(bmid - bc)).astype(BF16)
        k_state = (k * jnp.exp(blast - bc)).astype(BF16)
        decay = jnp.exp(blast)
        outs = []
        for h in range(heads):
            ks = slice(h * dk, (h + 1) * dk)
            vs = slice(h * dv, (h + 1) * dv)
            st = st_ref[h]
            vh = v[:, vs]
            inter = lax.dot_general(q_inter[:, ks], st.astype(BF16), nt, preferred_element_type=F32)
            att = lax.dot_general(q_intra[:, ks], k_intra[:, ks], nt, preferred_element_type=F32)
            att = jnp.where(tri, att, 0.0).astype(BF16)
            o = inter + jnp.dot(att, vh, preferred_element_type=F32)
            st_ref[h] = st * decay[:, ks] + lax.dot_general(vh, k_state[:, ks], tn,
                                                           preferred_element_type=F32)
            o = _rms(o, ng_ref[...])
            gh = g[:, vs]
            outs.append(o * (gh * _sigmoid(gh)))
        o_ref[sl, :] = jnp.concatenate(outs, axis=1).astype(o_ref.dtype)
        return carry

    lax.fori_loop(0, tb // c, body, 0)


def gla(z, col0, w_gk, b_gk, norm_g, hdk, hdv):
    b, tp, _ = z.shape
    tb = TIME_TILE
    cq, ck = col0 // hdk, col0 // hdk + 1
    cv = (col0 + 2 * hdk) // hdv
    cg = cv + 1
    cl = (col0 + 2 * hdk + 2 * hdv) // LANES
    full = lambda shape: pl.BlockSpec(shape, lambda bi, ti: (0,) * len(shape))
    return pl.pallas_call(
        functools.partial(_gla_kernel, chunk=GLA_CHUNK, heads=GLA_HEADS),
        out_shape=jax.ShapeDtypeStruct((b, tp, hdv), BF16),
        grid=(b, tp // tb),
        in_specs=[pl.BlockSpec((None, tb, hdk), lambda bi, ti: (bi, ti, cq)),
                  pl.BlockSpec((None, tb, hdk), lambda bi, ti: (bi, ti, ck)),
                  pl.BlockSpec((None, tb, hdv), lambda bi, ti: (bi, ti, cv)),
                  pl.BlockSpec((None, tb, hdv), lambda bi, ti: (bi, ti, cg)),
                  pl.BlockSpec((None, tb, LANES), lambda bi, ti: (bi, ti, cl)),
                  full((LANES, hdk)), full((1, hdk)), full((1, hdv // GLA_HEADS))],
        out_specs=pl.BlockSpec((None, tb, hdv), lambda bi, ti: (bi, ti, 0)),
        scratch_shapes=[pltpu.VMEM((GLA_HEADS, hdv // GLA_HEADS, hdk // GLA_HEADS), F32)],
        compiler_params=_cparams(("parallel", "arbitrary")),
        name="gla",
    )(z, z, z, z, z, w_gk, b_gk, norm_g)


def _rope_tile(x, cos, sin_up, sin_dn, half):
    return x * cos + pltpu.roll(x, half, 1) * sin_up + pltpu.roll(x, LANES - half, 1) * sin_dn


def _rope_kernel(qk_ref, qi_ref, kw_ref, tab_ref, qk_out, qi_out, ki_out, wi_out):
    def tabs(kind):
        return tab_ref[3 * kind], tab_ref[3 * kind + 1], tab_ref[3 * kind + 2]

    n_q_tiles = ATT_HEADS
    for t in range(qk_ref.shape[1] // LANES):
        ca, sa_up, sa_dn = tabs(0 if t < n_q_tiles else 1)
        sl = slice(t * LANES, (t + 1) * LANES)
        qk_out[:, sl] = _rope_tile(qk_ref[:, sl].astype(F32), ca, sa_up, sa_dn, 16).astype(BF16)
    cb, sb_up, sb_dn = tabs(2)
    for t in range(qi_ref.shape[1] // LANES):
        sl = slice(t * LANES, (t + 1) * LANES)
        qi_out[:, sl] = _rope_tile(qi_ref[:, sl].astype(F32), cb, sb_up, sb_dn, 8).astype(BF16)
    ck, sk_up, sk_dn = tabs(3)
    kw = _rope_tile(kw_ref[...].astype(F32), ck, sk_up, sk_dn, 8)
    ki_out[...] = kw.astype(BF16)
    wi_out[...] = pltpu.roll(kw, LANES - IDX_DIM, 1)


def rope_tables(tp, head_dim):
    pos = jnp.arange(tp, dtype=F32)[:, None]
    lane = jnp.arange(LANES)

    def tables(dim, period, active):
        half = (dim // 4) // 2
        freqs = ROPE_THETA ** (-jnp.arange(half, dtype=F32) / half)
        ang = pos * freqs[None, :]
        cos, sin = jnp.cos(ang), jnp.sin(ang)
        within = lane % period
        fidx = within % half
        cos_l, sin_l = cos[:, fidx], sin[:, fidx]
        lo = (within < half) & active
        hi = (within >= half) & (within < 2 * half) & active
        c = jnp.where((lo | hi)[None, :], cos_l, 1.0)
        s_up = jnp.where(hi[None, :], sin_l, 0.0)
        s_dn = jnp.where(lo[None, :], -sin_l, 0.0)
        return c, s_up, s_dn

    all_on = jnp.ones((LANES,), bool)
    a = tables(head_dim, head_dim, all_on)
    bq = tables(IDX_DIM, IDX_DIM, all_on)
    ck, sk_up, sk_dn = tables(IDX_DIM, IDX_DIM, lane < IDX_DIM)
    wscale = IDX_HEADS ** -0.5 * IDX_DIM ** -0.5
    ck = jnp.where((lane >= IDX_DIM)[None, :],
                   jnp.where(lane < IDX_DIM + IDX_HEADS, wscale, 0.0)[None, :], ck)
    a_scaled = [t * (head_dim ** -0.5 * LOG2_E) for t in a]
    return jnp.stack([*a_scaled, *a, *bq, ck, sk_up, sk_dn])


def rope(z, tabs, tp, nqk, nqi, col_qi, col_kw):
    m = z.shape[0]
    tm = TIME_TILE
    nper = tp // tm
    return pl.pallas_call(
        _rope_kernel,
        out_shape=[jax.ShapeDtypeStruct((m, nqk), BF16), jax.ShapeDtypeStruct((m, nqi), BF16),
                   jax.ShapeDtypeStruct((m, LANES), BF16), jax.ShapeDtypeStruct((m, LANES), F32)],
        grid=(m // tm,),
        in_specs=[pl.BlockSpec((tm, nqk), lambda i: (i, 0)),
                  pl.BlockSpec((tm, nqi), lambda i: (i, col_qi // nqi)),
                  pl.BlockSpec((tm, LANES), lambda i: (i, col_kw // LANES)),
                  pl.BlockSpec((12, tm, LANES), lambda i: (0, i % nper, 0))],
        out_specs=[pl.BlockSpec((tm, nqk), lambda i: (i, 0)),
                   pl.BlockSpec((tm, nqi), lambda i: (i, 0)),
                   pl.BlockSpec((tm, LANES), lambda i: (i, 0)),
                   pl.BlockSpec((tm, LANES), lambda i: (i, 0))],
        compiler_params=_cparams(("parallel",)),
        name="rope",
    )(z, z, z, tabs)


def _index_keys(qi_stack, w_rows, ki_blk, q0, k0):
    nt = (((1,), (1,)), ((), ()))
    tq = w_rows[0].shape[1]
    lg = lax.dot_general(ki_blk, qi_stack, nt, preferred_element_type=F32)
    score = None
    for h, wh in enumerate(w_rows):
        term = wh * jnp.maximum(lg[:, h * tq:(h + 1) * tq], 0.0)
        score = term if score is None else score + term
    tk = score.shape[0]
    kpos = k0 + lax.broadcasted_iota(I32, (tk, tq), 0)
    qpos = q0 + lax.broadcasted_iota(I32, (tk, tq), 1)
    score = jnp.where(kpos <= qpos, jnp.where(kpos < N_META_TOK, BIG, score), NEG)
    bits = pltpu.bitcast(score, I32)
    return jnp.where(bits < 0, bits ^ jnp.int32(0x7FFFFFFF), bits)


def _float_key(x):
    bits = int(np.float32(x).view(np.int32))
    return bits ^ 0x7FFFFFFF if bits < 0 else bits


def _indexer_kernel(qi_ref, wi_ref, ki_ref, bias_ref, keys_ref, qs_ref, cut_ref, *, k_sel, tk):
    tq = qi_ref.shape[0]
    qb = pl.program_id(1)
    q0 = qb * tq
    nkv = (q0 + tq + tk - 1) // tk
    for h in range(IDX_HEADS):
        qs_ref[h * tq:(h + 1) * tq, :] = qi_ref[:, h * IDX_DIM:(h + 1) * IDX_DIM]
    w_t = jnp.transpose(wi_ref[...])
    w_rows = [w_t[h:h + 1, :] for h in range(IDX_HEADS)]

    def fill(kb, carry):
        k0 = pl.multiple_of(kb * tk, tk)
        keys_ref[pl.ds(k0, tk), :] = _index_keys(qs_ref[...], w_rows,
                                                 ki_ref[pl.ds(k0, tk), 0:IDX_DIM], q0, k0)
        return carry

    lax.fori_loop(0, nkv, fill, 0)

    row = lax.broadcasted_iota(I32, (tk, tq), 0)

    def block(kb):
        k0 = pl.multiple_of(kb * tk, tk)
        return keys_ref[pl.ds(k0, tk), :], k0 + row

    def count(pred):
        def blk(kb, acc):
            hit = jnp.where(pred(*block(kb)), 1, 0)
            return acc + jnp.sum(hit.reshape(tk // SUBLANES, SUBLANES, tq), axis=0)
        acc = lax.fori_loop(0, nkv, blk, jnp.zeros((SUBLANES, tq), I32))
        return jnp.sum(acc, axis=0, keepdims=True)

    def bit_step(bi, thr):
        cand = thr + jnp.left_shift(jnp.int32(1), 31 - bi)
        n = count(lambda keys, kpos: keys >= cand)
        return jnp.where(n >= k_sel, cand, thr)

    thr = lax.fori_loop(0, 32, bit_step, jnp.full((1, tq), INT_MIN, I32))
    n_ge = count(lambda keys, kpos: keys >= thr)
    no_tie_cut = jnp.full((1, tq), 1 << 30, I32)
    cut_ref[...] = no_tie_cut

    @pl.when(jnp.max(n_ge) > k_sel)
    def _():
        n_gt = count(lambda keys, kpos: keys > thr)
        need = k_sel - n_gt

        def cut_step(bi, cut):
            cand = cut + jnp.left_shift(jnp.int32(1), 15 - bi)
            n = count(lambda keys, kpos: (keys == thr) & (kpos < cand))
            return jnp.where(n <= need, cand, cut)

        cut = lax.fori_loop(0, 16, cut_step, jnp.zeros((1, tq), I32))
        cut_ref[...] = jnp.where(n_ge > k_sel, cut, no_tie_cut)

    bias_ref[...] = jnp.full(bias_ref.shape, NEG, BF16)
    cut = cut_ref[...]
    key_valid = jnp.int32(_float_key(0.5 * NEG))

    def emit(kb, carry):
        k0 = pl.multiple_of(kb * tk, tk)
        keys, kpos = block(kb)
        chosen = ((keys > thr) | ((keys == thr) & (kpos < cut))) & (keys > key_valid)
        bias_ref[pl.ds(k0, tk), :] = jnp.where(chosen, 0.0, NEG).astype(BF16)
        return carry

    lax.fori_loop(0, nkv, emit, 0)


def indexer(qi, wi, ki, b, tp, k_sel):
    tq = INDEX_Q_TILE
    nq = tp // tq
    return pl.pallas_call(
        functools.partial(_indexer_kernel, k_sel=k_sel, tk=TIME_TILE),
        out_shape=jax.ShapeDtypeStruct((b * tp, tp), BF16),
        grid=(b, nq),
        in_specs=[pl.BlockSpec((tq, qi.shape[1]), lambda bi, qb: (bi * nq + qb, 0)),
                  pl.BlockSpec((tq, LANES), lambda bi, qb: (bi * nq + qb, 0)),
                  pl.BlockSpec((tp, LANES), lambda bi, qb: (bi, 0))],
        out_specs=pl.BlockSpec((tp, tq), lambda bi, qb: (bi, qb)),
        scratch_shapes=[pltpu.VMEM((tp, tq), I32), pltpu.VMEM((IDX_HEADS * tq, IDX_DIM), BF16),
                        pltpu.VMEM((1, tq), I32)],
        compiler_params=_cparams(("parallel", "arbitrary")),
        name="indexer",
    )(qi, wi, ki)


def _attn_kernel(qb_ref, kb_ref, q_ref, k_ref, v_ref, bias_ref, o_ref, qs_ref, m_ref, l_ref, acc_ref):
    tq = q_ref.shape[0]
    tk = k_ref.shape[0]
    hd = k_ref.shape[1] // ATT_KV_HEADS
    grows = ATT_GROUPS * tq
    qb = qb_ref[pl.program_id(1)]
    kb = kb_ref[pl.program_id(1)]
    nt = (((1,), (1,)), ((), ()))

    @pl.when(kb == 0)
    def _():
        m_ref[...] = jnp.full_like(m_ref, NEG)
        l_ref[...] = jnp.zeros_like(l_ref)
        acc_ref[...] = jnp.zeros_like(acc_ref)
        for hi in range(ATT_HEADS):
            qs_ref[hi * tq:(hi + 1) * tq, :] = q_ref[:, hi * hd:(hi + 1) * hd]

    bias = jnp.transpose(bias_ref[...].astype(F32))[None]
    ones = jnp.ones((tk, LANES), BF16)
    for kvh in range(ATT_KV_HEADS):
        rows = slice(kvh * grows, (kvh + 1) * grows)
        kh = k_ref[:, kvh * hd:(kvh + 1) * hd]
        vh1 = jnp.concatenate([v_ref[:, kvh * hd:(kvh + 1) * hd], ones], axis=1)
        s = lax.dot_general(qs_ref[rows, :], kh, nt, preferred_element_type=F32)
        s = (s.reshape(ATT_GROUPS, tq, tk) + bias).reshape(grows, tk)
        m_old = m_ref[rows, :]
        m_new = jnp.maximum(m_old, jnp.broadcast_to(jnp.max(s, axis=1, keepdims=True),
                                                    (grows, LANES)))
        p = jnp.concatenate(
            [jnp.exp2(s[:, t * LANES:(t + 1) * LANES] - m_new).astype(BF16)
             for t in range(tk // LANES)], axis=1)
        alpha = jnp.exp2(m_old - m_new)
        pv = jnp.dot(p, vh1, preferred_element_type=F32)
        l_ref[rows, :] = alpha * l_ref[rows, :] + pv[:, hd:]
        acc_ref[rows, :] = alpha * acc_ref[rows, :] + pv[:, :hd]
        m_ref[rows, :] = m_new

    @pl.when(kb == qb)
    def _():
        for hi in range(ATT_HEADS):
            rows = slice(hi * tq, (hi + 1) * tq)
            o_ref[:, hi * hd:(hi + 1) * hd] = (acc_ref[rows, :] / l_ref[rows, :]).astype(o_ref.dtype)


def sparse_attention(qk, z, bias, b, tp, d, col_v):
    tq = tk = TIME_TILE
    nq = tp // tq
    hd = d // ATT_HEADS
    kvw = ATT_KV_HEADS * hd
    pairs = [(qb, kb) for qb in range(nq) for kb in range(qb + 1)]
    qb_of = jnp.asarray([p[0] for p in pairs], I32)
    kb_of = jnp.asarray([p[1] for p in pairs], I32)
    qrow = lambda bi, i, qbs, kbs: bi * nq + qbs[i]
    kvrow = lambda bi, i, qbs, kbs: bi * nq + kbs[i]
    grid_spec = pltpu.PrefetchScalarGridSpec(
        num_scalar_prefetch=2,
        grid=(b, len(pairs)),
        in_specs=[pl.BlockSpec((tq, d), lambda bi, i, qbs, kbs: (qrow(bi, i, qbs, kbs), 0)),
                  pl.BlockSpec((tk, kvw), lambda bi, i, qbs, kbs: (kvrow(bi, i, qbs, kbs), d // kvw)),
                  pl.BlockSpec((tk, kvw),
                               lambda bi, i, qbs, kbs: (kvrow(bi, i, qbs, kbs), col_v // kvw)),
                  pl.BlockSpec((tk, tq), lambda bi, i, qbs, kbs: (kvrow(bi, i, qbs, kbs), qbs[i]))],
        out_specs=pl.BlockSpec((tq, d), lambda bi, i, qbs, kbs: (qrow(bi, i, qbs, kbs), 0)),
        scratch_shapes=[pltpu.VMEM((ATT_HEADS * tq, hd), BF16),
                        pltpu.VMEM((ATT_HEADS * tq, LANES), F32),
                        pltpu.VMEM((ATT_HEADS * tq, LANES), F32),
                        pltpu.VMEM((ATT_HEADS * tq, hd), F32)],
    )
    return pl.pallas_call(
        _attn_kernel,
        out_shape=jax.ShapeDtypeStruct((b * tp, d), BF16),
        grid_spec=grid_spec,
        compiler_params=_cparams(("parallel", "arbitrary")),
        name="sparse_attention",
    )(qb_of, kb_of, qk, qk, z, bias)


def _router_kernel(x_ref, g_ref, w_ref, o_ref, xt_ref):
    tm, d = x_ref.shape
    nchunk = d // LANES
    for c in range(nchunk):
        xt_ref[pl.ds(c, tm, stride=nchunk), :] = x_ref[:, c * LANES:(c + 1) * LANES]
    xn = _rms(x_ref[...], g_ref[...]).astype(BF16)
    logits = jnp.dot(xn, w_ref[...], preferred_element_type=F32)
    lane = lax.broadcasted_iota(I32, logits.shape, 1)
    logits = jnp.where(lane < N_EXPERTS, logits, -jnp.inf)
    v1 = jnp.max(logits, axis=1, keepdims=True)
    e1 = jnp.min(jnp.where(logits == v1, lane, LANES), axis=1, keepdims=True)
    rest = jnp.where(lane == e1, -jnp.inf, logits)
    v2 = jnp.max(rest, axis=1, keepdims=True)
    e2 = jnp.min(jnp.where(rest == v2, lane, LANES), axis=1, keepdims=True)
    ex = jnp.exp(v2 - v1)
    g1 = 1.0 / (1.0 + ex)
    g2 = ex / (1.0 + ex)
    out = jnp.where(lane == 0, g1, jnp.where(lane == 1, g2, 0.0))
    out = jnp.where(lane == 2, e1.astype(F32), jnp.where(lane == 3, e2.astype(F32), out))
    o_ref[...] = out


def router(x, g, w):
    m, d = x.shape
    tm = TIME_TILE
    nchunk = d // LANES
    return pl.pallas_call(
        _router_kernel,
        out_shape=[jax.ShapeDtypeStruct((m, LANES), F32),
                   jax.ShapeDtypeStruct((m * nchunk, LANES), F32)],
        grid=(m // tm,),
        in_specs=[pl.BlockSpec((tm, d), lambda i: (i, 0)),
                  pl.BlockSpec((1, d), lambda i: (0, 0)),
                  pl.BlockSpec((d, LANES), lambda i: (0, 0))],
        out_specs=[pl.BlockSpec((tm, LANES), lambda i: (i, 0)),
                   pl.BlockSpec((tm * nchunk, LANES), lambda i: (i, 0))],
        compiler_params=_cparams(("parallel",)),
        name="router",
    )(x, g, w)


def _combine_norm_kernel(slots_ref, h_ref, y_hbm, route_ref, g_ref, o_ref, ybuf_ref, sem_ref):
    i = pl.program_id(0)
    nblk = pl.num_programs(0)
    tm, d = h_ref.shape
    nchunk = d // LANES
    m = nblk * tm

    def gather(blk):
        slot = blk % 2
        first = _token_gather(slots_ref, blk * tm, tm, y_hbm, ybuf_ref.at[slot, 0],
                              sem_ref.at[slot], nchunk)
        second = _token_gather(slots_ref, m + blk * tm, tm, y_hbm, ybuf_ref.at[slot, 1],
                               sem_ref.at[slot], nchunk)
        return (lambda: (first[0](), second[0]())), (lambda: (first[1](), second[1]()))

    @pl.when(i == 0)
    def _():
        gather(i)[0]()

    gather(i)[1]()

    @pl.when(i + 1 < nblk)
    def _():
        gather(i + 1)[0]()

    y1_ref = ybuf_ref.at[i % 2, 0]
    y2_ref = ybuf_ref.at[i % 2, 1]
    g1 = route_ref[:, 0:1]
    g2 = route_ref[:, 1:2]
    ssq = jnp.zeros((tm, LANES), F32)
    for c in range(nchunk):
        sl = slice(c * LANES, (c + 1) * LANES)
        hc = (h_ref[:, sl] + g1 * _chunk_rows(y1_ref, c, tm, nchunk)
              + g2 * _chunk_rows(y2_ref, c, tm, nchunk))
        o_ref[:, sl] = hc
        ssq = ssq + hc * hc
    inv = lax.rsqrt(jnp.sum(ssq, axis=1, keepdims=True) / d + EPS)
    o_ref[...] = o_ref[...] * inv * g_ref[...]


def combine_norm(h, y_slots, tok_slots, route, g):
    m, d = h.shape
    tm = TIME_TILE
    nchunk = d // LANES
    grid_spec = pltpu.PrefetchScalarGridSpec(
        num_scalar_prefetch=1,
        grid=(m // tm,),
        in_specs=[pl.BlockSpec((tm, d), lambda i, s: (i, 0)),
                  pl.BlockSpec(memory_space=pl.ANY),
                  pl.BlockSpec((tm, LANES), lambda i, s: (i, 0)),
                  pl.BlockSpec((1, d), lambda i, s: (0, 0))],
        out_specs=pl.BlockSpec((tm, d), lambda i, s: (i, 0)),
        scratch_shapes=[pltpu.VMEM((2, 2, tm * nchunk, LANES), F32), pltpu.SemaphoreType.DMA((2,))],
    )
    return pl.pallas_call(
        _combine_norm_kernel,
        out_shape=jax.ShapeDtypeStruct((m, d), F32),
        grid_spec=grid_spec,
        compiler_params=_cparams(("arbitrary",)),
        name="combine_norm",
    )(tok_slots, h, y_slots, route, g)


def _pad_cols(w, n):
    return jnp.pad(w, ((0, 0), (0, n - w.shape[1])))


def _round_up(n, k):
    return -(-n // k) * k


def layer_ab(h, b, tp, norm_g, w_in, conv_w, conv_b, w_r, b_r, w_i, b_i, lam, w_gk, b_gk,
             gla_norm, w_out, ffn_norm, ffn_wg, ffn_wu, ffn_wd):
    m, d = h.shape
    lw = conv_w.shape[1]
    hdk = w_gk.shape[1]
    hdv = d - lw
    n_in = _round_up(w_in.shape[1] - GLA_RANK + LANES, COL_TILE)
    z = norm_matmul(h, norm_g[None], _pad_cols(w_in, n_in).astype(BF16)).reshape(b, tp, n_in)
    a_out = rg_lru(z, conv_w, conv_b[None], w_r.astype(BF16), b_r.reshape(1, lw),
                   w_i.astype(BF16), b_i.reshape(1, lw), lam[None])
    w_gk_p = jnp.pad(w_gk, ((0, LANES - GLA_RANK), (0, 0))).astype(BF16)
    b_out = gla(z, 2 * lw, w_gk_p, b_gk[None], gla_norm[None], hdk, hdv)
    h = matmul_residual([a_out.reshape(m, lw), b_out.reshape(m, hdv)], w_out.astype(BF16), h)
    return ffn_residual(h, ffn_norm[None], ffn_wg.astype(BF16)[None], ffn_wu.astype(BF16)[None],
                        ffn_wd.astype(BF16)[None])


def layer_c(h, b, tp, t_real, norm_g, w_in, w_out):
    m, d = h.shape
    hd = d // ATT_HEADS
    kvw = ATT_KV_HEADS * hd
    nqi = IDX_HEADS * IDX_DIM
    col_v = d + kvw
    col_qi = d + 2 * kvw
    col_kw = col_qi + nqi
    n_in = _round_up(col_kw + LANES, COL_TILE)
    z = norm_matmul(h, norm_g[None], _pad_cols(w_in, n_in).astype(BF16))
    qk, qi, ki, wi = rope(z, rope_tables(tp, hd), tp, d + kvw, nqi, col_qi, col_kw)
    k_sel = min(TOPK_MAX, (t_real - N_META_TOK) // 4)
    bias = indexer(qi, wi, ki, b, tp, k_sel)
    att = sparse_attention(qk, z, bias, b, tp, d, col_v)
    return matmul_residual([att], w_out.astype(BF16), h)


def moe_dispatch(route, tm):
    m = route.shape[0]
    e_flat = route[:, 2:4].astype(I32).reshape(-1)
    onehot = (e_flat[:, None] == jnp.arange(N_EXPERTS)[None, :]).astype(I32)
    rank = jnp.take_along_axis(jnp.cumsum(onehot, axis=0) - onehot, e_flat[:, None], axis=1)[:, 0]
    counts = jnp.sum(onehot, axis=0)
    padded = (counts + tm - 1) // tm * tm
    pad_end = jnp.cumsum(padded)
    dest = (pad_end - padded)[e_flat] + rank
    nblk = -(-(2 * m) // tm) + N_EXPERTS
    slot_tok = jnp.zeros((nblk * tm,), I32).at[dest].set(jnp.arange(2 * m, dtype=I32) // 2)
    nused = (pad_end[-1] // tm).astype(I32)
    blk = jnp.minimum(jnp.arange(nblk, dtype=I32), nused - 1)
    blk_expert = jnp.minimum(jnp.sum((pad_end[None, :] <= (blk * tm)[:, None]).astype(I32), axis=1),
                             N_EXPERTS - 1)
    return slot_tok, dest.reshape(m, 2), blk_expert, nused.reshape(1)


def layer_moe_final(h, norm_g, w_router, wg, wu, wd, final_g):
    m = h.shape[0]
    route, h_tok = router(h, norm_g[None], _pad_cols(w_router, LANES).astype(BF16))
    tm = _pick_tile(m, (ROW_TILE_BIG, TIME_TILE))
    slot_tok, tok_slots, blk_expert, nused = moe_dispatch(route, tm)
    y_slots = ffn_experts(h_tok, slot_tok, norm_g[None], wg.astype(BF16), wu.astype(BF16),
                          wd.astype(BF16), blk_expert, nused, tm)
    return combine_norm(h, y_slots, tok_slots.T.reshape(-1), route, final_g[None])


def kernel(x, meta, ab_norm, ab_w_in, lru_conv_w, lru_conv_b, lru_w_r, lru_b_r, lru_w_i, lru_b_i,
           lru_lam, gla_w_gk, gla_b_gk, gla_norm, ab_w_out, ffn_norm, ffn_w_gate, ffn_w_up,
           ffn_w_down, c_norm, c_w_in, c_w_out, moe_norm, moe_router, moe_w_gate, moe_w_up,
           moe_w_down, final_norm):
    b, seq, d = x.shape
    assert ab_norm.shape[0] == 1 and c_norm.shape[0] == 1, "two-layer trunk only"
    t_real = seq + N_META_TOK
    tp = _round_up(t_real, TIME_TILE)
    h = jnp.concatenate([jnp.broadcast_to(meta.astype(x.dtype)[None], (b, N_META_TOK, d)), x,
                         jnp.zeros((b, tp - t_real, d), x.dtype)], axis=1).reshape(b * tp, d)
    h = layer_ab(h, b, tp, ab_norm[0], ab_w_in[0], lru_conv_w[0], lru_conv_b[0], lru_w_r[0],
                 lru_b_r[0], lru_w_i[0], lru_b_i[0], lru_lam[0], gla_w_gk[0], gla_b_gk[0],
                 gla_norm[0], ab_w_out[0], ffn_norm[0], ffn_w_gate[0], ffn_w_up[0], ffn_w_down[0])
    h = layer_c(h, b, tp, t_real, c_norm[0], c_w_in[0], c_w_out[0])
    out = layer_moe_final(h, moe_norm[0], moe_router[0], moe_w_gate[0], moe_w_up[0],
                          moe_w_down[0], final_norm)
    return out.reshape(b, tp, d)[:, N_META_TOK:t_real]
```

```python
import functools

import jax
import jax.numpy as jnp
import numpy as np
from jax import lax
from jax.experimental import pallas as pl
from jax.experimental.pallas import tpu as pltpu

F32 = jnp.float32
BF16 = jnp.bfloat16
I32 = jnp.int32

N_META_TOK = 16
EPS = 1e-6
ROPE_THETA = 500000.0
LRU_BLOCKS = 4
CONV_W = 4
LRU_C = 8.0
GLA_HEADS = 4
GLA_RANK = 16
GLA_GATE_NORM = 16.0
ATT_HEADS = 16
ATT_KV_HEADS = 4
ATT_GROUPS = ATT_HEADS // ATT_KV_HEADS
IDX_HEADS = 8
IDX_DIM = 64
TOPK_MAX = 256
BIG = 1e30
NEG = -1e30
N_EXPERTS = 8
LOG2_E = 1.4426950408889634

LANES = 128
SUBLANES = 8
TIME_TILE = 384
INDEX_Q_TILE = 128
ROW_TILE_BIG = 768
COL_TILE = 512
GLA_CHUNK = 64
VMEM_LIMIT = 56 * 1024 * 1024
INT_MIN = -(2 ** 31)


def _cparams(sem):
    return pltpu.CompilerParams(dimension_semantics=sem, vmem_limit_bytes=VMEM_LIMIT)


def _rms(x, g):
    return x * lax.rsqrt(jnp.mean(x * x, axis=-1, keepdims=True) + EPS) * g


def _sigmoid(x):
    return 1.0 / (1.0 + jnp.exp(-x))


def _pick_tile(n, candidates):
    for c in candidates:
        if n % c == 0:
            return c
    raise ValueError(f"no tile in {candidates} divides {n}")


def _norm_matmul_kernel(x_ref, g_ref, w_ref, o_ref, xn_ref):
    @pl.when(pl.program_id(1) == 0)
    def _():
        xn_ref[...] = _rms(x_ref[...], g_ref[...]).astype(BF16)

    o_ref[...] = jnp.dot(xn_ref[...], w_ref[...], preferred_element_type=F32).astype(o_ref.dtype)


def norm_matmul(x, g, w):
    m, d = x.shape
    n = w.shape[1]
    tm = _pick_tile(m, (ROW_TILE_BIG, TIME_TILE))
    tn = COL_TILE
    return pl.pallas_call(
        _norm_matmul_kernel,
        out_shape=jax.ShapeDtypeStruct((m, n), BF16),
        grid=(m // tm, n // tn),
        in_specs=[pl.BlockSpec((tm, d), lambda i, j: (i, 0)),
                  pl.BlockSpec((1, d), lambda i, j: (0, 0)),
                  pl.BlockSpec((d, tn), lambda i, j: (0, j))],
        out_specs=pl.BlockSpec((tm, tn), lambda i, j: (i, j)),
        scratch_shapes=[pltpu.VMEM((tm, d), BF16)],
        compiler_params=_cparams(("parallel", "arbitrary")),
        name="norm_matmul",
    )(x, g, w)


def _matmul_res_kernel(*refs, nparts):
    xs = refs[:nparts]
    w_ref, r_ref, o_ref = refs[nparts:]
    acc = r_ref[...]
    k0 = 0
    for x_ref in xs:
        kp = x_ref.shape[1]
        acc = acc + jnp.dot(x_ref[...], w_ref[k0:k0 + kp, :], preferred_element_type=F32)
        k0 += kp
    o_ref[...] = acc


def matmul_residual(parts, w, res):
    m, n = res.shape
    k = w.shape[0]
    tm = _pick_tile(m, (ROW_TILE_BIG, TIME_TILE))
    tn = 1024
    in_specs = [pl.BlockSpec((tm, p.shape[1]), lambda i, j: (i, 0)) for p in parts]
    in_specs += [pl.BlockSpec((k, tn), lambda i, j: (0, j)),
                 pl.BlockSpec((tm, tn), lambda i, j: (i, j))]
    return pl.pallas_call(
        functools.partial(_matmul_res_kernel, nparts=len(parts)),
        out_shape=jax.ShapeDtypeStruct((m, n), F32),
        grid=(m // tm, n // tn),
        in_specs=in_specs,
        out_specs=pl.BlockSpec((tm, tn), lambda i, j: (i, j)),
        compiler_params=_cparams(("parallel", "arbitrary")),
        name="matmul_residual",
    )(*parts, w, res)


def _chunk_rows(ref, c, rows, nchunk):
    return ref[pl.ds(c, rows, stride=nchunk), :]


def _swiglu_partial(xn_ref, wg_ref, wu_ref, wd_ref):
    xn = xn_ref[...]
    gate = jnp.dot(xn, wg_ref[0], preferred_element_type=F32)
    up = jnp.dot(xn, wu_ref[0], preferred_element_type=F32)
    act = (gate * _sigmoid(gate) * up).astype(BF16)
    return jnp.dot(act, wd_ref[0], preferred_element_type=F32)


def _ffn_kernel(x_ref, g_ref, wg_ref, wu_ref, wd_ref, o_ref, xn_ref):
    j = pl.program_id(1)

    @pl.when(j == 0)
    def _():
        xn_ref[...] = _rms(x_ref[...], g_ref[...]).astype(BF16)

    part = _swiglu_partial(xn_ref, wg_ref, wu_ref, wd_ref)

    @pl.when(j == 0)
    def _():
        o_ref[...] = part + x_ref[...]

    @pl.when(j > 0)
    def _():
        o_ref[...] += part


def ffn_residual(x, g, wg, wu, wd):
    m, d = x.shape
    f = wg.shape[2]
    tm = _pick_tile(m, (ROW_TILE_BIG, TIME_TILE))
    tf = COL_TILE
    return pl.pallas_call(
        _ffn_kernel,
        out_shape=jax.ShapeDtypeStruct((m, d), F32),
        grid=(m // tm, f // tf),
        in_specs=[pl.BlockSpec((tm, d), lambda i, j: (i, 0)),
                  pl.BlockSpec((1, d), lambda i, j: (0, 0)),
                  pl.BlockSpec((1, d, tf), lambda i, j: (0, 0, j)),
                  pl.BlockSpec((1, d, tf), lambda i, j: (0, 0, j)),
                  pl.BlockSpec((1, tf, d), lambda i, j: (0, j, 0))],
        out_specs=pl.BlockSpec((tm, d), lambda i, j: (i, 0)),
        scratch_shapes=[pltpu.VMEM((tm, d), BF16)],
        compiler_params=_cparams(("parallel", "arbitrary")),
        name="ffn_residual",
    )(x, g, wg, wu, wd)


def _token_gather(idx_ref, first, count, src_ref, dst_ref, sem, nchunk):
    def copy(r):
        src_row = pl.multiple_of(idx_ref[first + r] * nchunk, nchunk)
        dst_row = pl.multiple_of(r * nchunk, nchunk)
        return pltpu.make_async_copy(src_ref.at[pl.ds(src_row, nchunk)],
                                     dst_ref.at[pl.ds(dst_row, nchunk)], sem)

    def start():
        lax.fori_loop(0, count, lambda r, c: (copy(r).start(), c)[1], 0)

    def wait():
        rows = count * nchunk
        pltpu.make_async_copy(src_ref.at[pl.ds(0, rows)], dst_ref.at[pl.ds(0, rows)], sem).wait()

    return start, wait


def _experts_kernel(eid_ref, nused_ref, tok_ref, x_hbm, g_ref, wg_ref, wu_ref, wd_ref, o_ref,
                    xbuf_ref, sem_ref, xn_ref, acc_ref):
    i = pl.program_id(0)
    j = pl.program_id(1)
    tm, d = xn_ref.shape
    nchunk = d // LANES
    nused = nused_ref[0]

    def gather(blk):
        slot = blk % 2
        return _token_gather(tok_ref, blk * tm, tm, x_hbm, xbuf_ref.at[slot], sem_ref.at[slot],
                             nchunk)

    @pl.when(i < nused)
    def _():
        @pl.when(j == 0)
        def _():
            @pl.when(i == 0)
            def _():
                gather(i)[0]()

            gather(i)[1]()

            @pl.when(i + 1 < nused)
            def _():
                gather(i + 1)[0]()

            x_ref = xbuf_ref.at[i % 2]
            ssq = jnp.zeros((tm, LANES), F32)
            for c in range(nchunk):
                xc = _chunk_rows(x_ref, c, tm, nchunk)
                ssq = ssq + xc * xc
            inv = lax.rsqrt(jnp.sum(ssq, axis=1, keepdims=True) / d + EPS)
            for c in range(nchunk):
                sl = slice(c * LANES, (c + 1) * LANES)
                xn_ref[:, sl] = (_chunk_rows(x_ref, c, tm, nchunk) * inv * g_ref[:, sl]).astype(BF16)

        part = _swiglu_partial(xn_ref, wg_ref, wu_ref, wd_ref)

        @pl.when(j == 0)
        def _():
            acc_ref[...] = part

        @pl.when(j > 0)
        def _():
            acc_ref[...] += part

    @pl.when(j == pl.num_programs(1) - 1)
    def _():
        @pl.when(i >= nused_ref[0])
        def _():
            acc_ref[...] = jnp.zeros_like(acc_ref)

        for c in range(nchunk):
            o_ref[pl.ds(c, tm, stride=nchunk), :] = acc_ref[:, c * LANES:(c + 1) * LANES]


def ffn_experts(x_tok, slot_tok, g, wg, wu, wd, eid, nused, tm):
    d = wg.shape[1]
    f = wg.shape[2]
    nchunk = d // LANES
    nblk = slot_tok.shape[0] // tm
    tf = COL_TILE
    nj = f // tf

    def jj(i, j, nused):
        return jnp.where(i < nused[0], j, nj - 1)

    grid_spec = pltpu.PrefetchScalarGridSpec(
        num_scalar_prefetch=3,
        grid=(nblk, nj),
        in_specs=[pl.BlockSpec(memory_space=pl.ANY),
                  pl.BlockSpec((1, d), lambda i, j, eid, nu, tok: (0, 0)),
                  pl.BlockSpec((1, d, tf), lambda i, j, eid, nu, tok: (eid[i], 0, jj(i, j, nu))),
                  pl.BlockSpec((1, d, tf), lambda i, j, eid, nu, tok: (eid[i], 0, jj(i, j, nu))),
                  pl.BlockSpec((1, tf, d), lambda i, j, eid, nu, tok: (eid[i], jj(i, j, nu), 0))],
        out_specs=pl.BlockSpec((tm * nchunk, LANES), lambda i, j, eid, nu, tok: (i, 0)),
        scratch_shapes=[pltpu.VMEM((2, tm * nchunk, LANES), F32), pltpu.SemaphoreType.DMA((2,)),
                        pltpu.VMEM((tm, d), BF16), pltpu.VMEM((tm, d), F32)],
    )
    return pl.pallas_call(
        _experts_kernel,
        out_shape=jax.ShapeDtypeStruct((nblk * tm * nchunk, LANES), F32),
        grid_spec=grid_spec,
        compiler_params=_cparams(("arbitrary", "arbitrary")),
        name="ffn_experts",
    )(eid, nused, slot_tok, x_tok, g, wg, wu, wd)


def _lru_kernel(xb_ref, yb_ref, cw_ref, cb_ref, wr_ref, br_ref, wi_ref, bi_ref, lam_ref,
                o_ref, xext_ref, h_ref):
    tb, w = xb_ref.shape

    @pl.when(pl.program_id(1) == 0)
    def _():
        xext_ref[0:SUBLANES, :] = jnp.zeros((SUBLANES, w), F32)
        h_ref[...] = jnp.zeros_like(h_ref)

    xext_ref[SUBLANES:SUBLANES + tb, :] = xb_ref[...].astype(F32)
    cw = cw_ref[...]
    xc = cb_ref[...] + cw[CONV_W - 1:CONV_W, :] * xext_ref[SUBLANES:SUBLANES + tb, :]
    for j in range(CONV_W - 1):
        xc = xc + cw[j:j + 1, :] * xext_ref[pl.ds(SUBLANES - (CONV_W - 1) + j, tb), :]
    xext_ref[0:SUBLANES, :] = xext_ref[tb:tb + SUBLANES, :]

    xcb = xc.astype(BF16)
    nb = wr_ref.shape[0]
    bw = w // nb
    r_parts, i_parts = [], []
    for n in range(nb):
        xs = xcb[:, n * bw:(n + 1) * bw]
        r_parts.append(jnp.dot(xs, wr_ref[n], preferred_element_type=F32))
        i_parts.append(jnp.dot(xs, wi_ref[n], preferred_element_type=F32))
    r = _sigmoid(jnp.concatenate(r_parts, axis=1) + br_ref[...])
    ig = _sigmoid(jnp.concatenate(i_parts, axis=1) + bi_ref[...])

    neg_lam = -lam_ref[...]
    softplus = jnp.maximum(neg_lam, 0.0) + jnp.log1p(jnp.exp(-jnp.abs(neg_lam)))
    log_a = -LRU_C * r * softplus
    a = jnp.exp(log_a)
    u = jnp.sqrt(1.0 - a * a) * (ig * xc)

    row = lax.broadcasted_iota(I32, (tb, w), 0)
    s = 1
    while s < tb:
        keep = row >= s
        a_sh = pltpu.roll(a, s, 0)
        u_sh = pltpu.roll(u, s, 0)
        u = jnp.where(keep, a * u_sh + u, u)
        a = jnp.where(keep, a * a_sh, a)
        s *= 2
    h = a * h_ref[0:1, :] + u
    h_ref[0:1, :] = h[tb - 1:tb, :]

    y = yb_ref[...].astype(F32)
    gelu = 0.5 * y * (1.0 + jnp.tanh(0.7978845608028654 * (y + 0.044715 * (y * y * y))))
    o_ref[...] = (h * gelu).astype(o_ref.dtype)


def rg_lru(z, conv_w, conv_b, w_r, b_r, w_i, b_i, lam):
    b, tp, _ = z.shape
    w = conv_w.shape[1]
    tb = TIME_TILE
    nb = w_r.shape[0]
    bw = w // nb
    full = lambda shape: pl.BlockSpec(shape, lambda bi, ti: (0,) * len(shape))
    return pl.pallas_call(
        _lru_kernel,
        out_shape=jax.ShapeDtypeStruct((b, tp, w), BF16),
        grid=(b, tp // tb),
        in_specs=[pl.BlockSpec((None, tb, w), lambda bi, ti: (bi, ti, 0)),
                  pl.BlockSpec((None, tb, w), lambda bi, ti: (bi, ti, 1)),
                  full((CONV_W, w)), full((1, w)),
                  full((nb, bw, bw)), full((1, w)),
                  full((nb, bw, bw)), full((1, w)),
                  full((1, w))],
        out_specs=pl.BlockSpec((None, tb, w), lambda bi, ti: (bi, ti, 0)),
        scratch_shapes=[pltpu.VMEM((tb + 2 * SUBLANES, w), F32), pltpu.VMEM((SUBLANES, w), F32)],
        compiler_params=_cparams(("parallel", "arbitrary")),
        name="rg_lru",
    )(z, z, conv_w, conv_b, w_r, b_r, w_i, b_i, lam)


def _gla_kernel(q_ref, k_ref, v_ref, g_ref, glr_ref, wgk_ref, bgk_ref, ng_ref, o_ref, st_ref,
                *, chunk, heads):
    tb, hdk = q_ref.shape
    dk = hdk // heads
    dv = v_ref.shape[1] // heads
    c = chunk

    @pl.when(pl.program_id(1) == 0)
    def _():
        st_ref[...] = jnp.zeros_like(st_ref)

    row = lax.broadcasted_iota(I32, (c, hdk), 0)
    tri = lax.broadcasted_iota(I32, (c, c), 0) >= lax.broadcasted_iota(I32, (c, c), 1)
    nt = (((1,), (1,)), ((), ()))
    tn = (((0,), (0,)), ((), ()))

    def body(ci, carry):
        sl = pl.ds(pl.multiple_of(ci * c, c), c)
        pre = jnp.dot(glr_ref[sl, :], wgk_ref[...], preferred_element_type=F32) + bgk_ref[...]
        gk = (jnp.minimum(pre, 0.0) - jnp.log1p(jnp.exp(-jnp.abs(pre)))) * (1.0 / GLA_GATE_NORM)
        bc = gk
        s = 1
        while s < c:
            bc = bc + jnp.where(row >= s, pltpu.roll(bc, s, 0), 0.0)
            s *= 2
        q = q_ref[sl, :].astype(F32) * (dk ** -0.5)
        k = k_ref[sl, :].astype(F32)
        v = v_ref[sl, :]
        g = g_ref[sl, :].astype(F32)
        bmid = bc[c // 2:c // 2 + 1, :]
        blast = bc[c - 1:c, :]
        q_inter = (q * jnp.exp(bc)).astype(BF16)
        q_intra = (q * jnp.exp(bc - bmid)).astype(BF16)
        k_intra = (k * jnp.exp(bmid - bc)).astype(BF16)
        k_state = (k * jnp.exp(blast - bc)).astype(BF16)
        decay = jnp.exp(blast)
        outs = []
        for h in range(heads):
            ks = slice(h * dk, (h + 1) * dk)
            vs = slice(h * dv, (h + 1) * dv)
            st = st_ref[h]
            vh = v[:, vs]
            inter = lax.dot_general(q_inter[:, ks], st.astype(BF16), nt, preferred_element_type=F32)
            att = lax.dot_general(q_intra[:, ks], k_intra[:, ks], nt, preferred_element_type=F32)
            att = jnp.where(tri, att, 0.0).astype(BF16)
            o = inter + jnp.dot(att, vh, preferred_element_type=F32)
            st_ref[h] = st * decay[:, ks] + lax.dot_general(vh, k_state[:, ks], tn,
                                                           preferred_element_type=F32)
            o = _rms(o, ng_ref[...])
            gh = g[:, vs]
            outs.append(o * (gh * _sigmoid(gh)))
        o_ref[sl, :] = jnp.concatenate(outs, axis=1).astype(o_ref.dtype)
        return carry

    lax.fori_loop(0, tb // c, body, 0)


def gla(z, col0, w_gk, b_gk, norm_g, hdk, hdv):
    b, tp, _ = z.shape
    tb = TIME_TILE
    cq, ck = col0 // hdk, col0 // hdk + 1
    cv = (col0 + 2 * hdk) // hdv
    cg = cv + 1
    cl = (col0 + 2 * hdk + 2 * hdv) // LANES
    full = lambda shape: pl.BlockSpec(shape, lambda bi, ti: (0,) * len(shape))
    return pl.pallas_call(
        functools.partial(_gla_kernel, chunk=GLA_CHUNK, heads=GLA_HEADS),
        out_shape=jax.ShapeDtypeStruct((b, tp, hdv), BF16),
        grid=(b, tp // tb),
        in_specs=[pl.BlockSpec((None, tb, hdk), lambda bi, ti: (bi, ti, cq)),
                  pl.BlockSpec((None, tb, hdk), lambda bi, ti: (bi, ti, ck)),
                  pl.BlockSpec((None, tb, hdv), lambda bi, ti: (bi, ti, cv)),
                  pl.BlockSpec((None, tb, hdv), lambda bi, ti: (bi, ti, cg)),
                  pl.BlockSpec((None, tb, LANES), lambda bi, ti: (bi, ti, cl)),
                  full((LANES, hdk)), full((1, hdk)), full((1, hdv // GLA_HEADS))],
        out_specs=pl.BlockSpec((None, tb, hdv), lambda bi, ti: (bi, ti, 0)),
        scratch_shapes=[pltpu.VMEM((GLA_HEADS, hdv // GLA_HEADS, hdk // GLA_HEADS), F32)],
        compiler_params=_cparams(("parallel", "arbitrary")),
        name="gla",
    )(z, z, z, z, z, w_gk, b_gk, norm_g)


def _rope_tile(x, cos, sin_up, sin_dn, half):
    return x * cos + pltpu.roll(x, half, 1) * sin_up + pltpu.roll(x, LANES - half, 1) * sin_dn


def _rope_kernel(qk_ref, qi_ref, kw_ref, tab_ref, qk_out, qi_out, ki_out, wi_out):
    def tabs(kind):
        return tab_ref[3 * kind], tab_ref[3 * kind + 1], tab_ref[3 * kind + 2]

    n_q_tiles = ATT_HEADS
    for t in range(qk_ref.shape[1] // LANES):
        ca, sa_up, sa_dn = tabs(0 if t < n_q_tiles else 1)
        sl = slice(t * LANES, (t + 1) * LANES)
        qk_out[:, sl] = _rope_tile(qk_ref[:, sl].astype(F32), ca, sa_up, sa_dn, 16).astype(BF16)
    cb, sb_up, sb_dn = tabs(2)
    for t in range(qi_ref.shape[1] // LANES):
        sl = slice(t * LANES, (t + 1) * LANES)
        qi_out[:, sl] = _rope_tile(qi_ref[:, sl].astype(F32), cb, sb_up, sb_dn, 8).astype(BF16)
    ck, sk_up, sk_dn = tabs(3)
    kw = _rope_tile(kw_ref[...].astype(F32), ck, sk_up, sk_dn, 8)
    ki_out[...] = kw.astype(BF16)
    wi_out[...] = pltpu.roll(kw, LANES - IDX_DIM, 1)


def rope_tables(tp, head_dim):
    pos = jnp.arange(tp, dtype=F32)[:, None]
    lane = jnp.arange(LANES)

    def tables(dim, period, active):
        half = (dim // 4) // 2
        freqs = ROPE_THETA ** (-jnp.arange(half, dtype=F32) / half)
        ang = pos * freqs[None, :]
        cos, sin = jnp.cos(ang), jnp.sin(ang)
        within = lane % period
        fidx = within % half
        cos_l, sin_l = cos[:, fidx], sin[:, fidx]
        lo = (within < half) & active
        hi = (within >= half) & (within < 2 * half) & active
        c = jnp.where((lo | hi)[None, :], cos_l, 1.0)
        s_up = jnp.where(hi[None, :], sin_l, 0.0)
        s_dn = jnp.where(lo[None, :], -sin_l, 0.0)
        return c, s_up, s_dn

    all_on = jnp.ones((LANES,), bool)
    a = tables(head_dim, head_dim, all_on)
    bq = tables(IDX_DIM, IDX_DIM, all_on)
    ck, sk_up, sk_dn = tables(IDX_DIM, IDX_DIM, lane < IDX_DIM)
    wscale = IDX_HEADS ** -0.5 * IDX_DIM ** -0.5
    ck = jnp.where((lane >= IDX_DIM)[None, :],
                   jnp.where(lane < IDX_DIM + IDX_HEADS, wscale, 0.0)[None, :], ck)
    a_scaled = [t * (head_dim ** -0.5 * LOG2_E) for t in a]
    return jnp.stack([*a_scaled, *a, *bq, ck, sk_up, sk_dn])


def rope(z, tabs, tp, nqk, nqi, col_qi, col_kw):
    m = z.shape[0]
    tm = TIME_TILE
    nper = tp // tm
    return pl.pallas_call(
        _rope_kernel,
        out_shape=[jax.ShapeDtypeStruct((m, nqk), BF16), jax.ShapeDtypeStruct((m, nqi), BF16),
                   jax.ShapeDtypeStruct((m, LANES), BF16), jax.ShapeDtypeStruct((m, LANES), F32)],
        grid=(m // tm,),
        in_specs=[pl.BlockSpec((tm, nqk), lambda i: (i, 0)),
                  pl.BlockSpec((tm, nqi), lambda i: (i, col_qi // nqi)),
                  pl.BlockSpec((tm, LANES), lambda i: (i, col_kw // LANES)),
                  pl.BlockSpec((12, tm, LANES), lambda i: (0, i % nper, 0))],
        out_specs=[pl.BlockSpec((tm, nqk), lambda i: (i, 0)),
                   pl.BlockSpec((tm, nqi), lambda i: (i, 0)),
                   pl.BlockSpec((tm, LANES), lambda i: (i, 0)),
                   pl.BlockSpec((tm, LANES), lambda i: (i, 0))],
        compiler_params=_cparams(("parallel",)),
        name="rope",
    )(z, z, z, tabs)


def _index_keys(qi_stack, w_rows, ki_blk, q0, k0):
    nt = (((1,), (1,)), ((), ()))
    tq = w_rows[0].shape[1]
    lg = lax.dot_general(ki_blk, qi_stack, nt, preferred_element_type=F32)
    score = None
    for h, wh in enumerate(w_rows):
        term = wh * jnp.maximum(lg[:, h * tq:(h + 1) * tq], 0.0)
        score = term if score is None else score + term
    tk = score.shape[0]
    kpos = k0 + lax.broadcasted_iota(I32, (tk, tq), 0)
    qpos = q0 + lax.broadcasted_iota(I32, (tk, tq), 1)
    score = jnp.where(kpos <= qpos, jnp.where(kpos < N_META_TOK, BIG, score), NEG)
    bits = pltpu.bitcast(score, I32)
    return jnp.where(bits < 0, bits ^ jnp.int32(0x7FFFFFFF), bits)


def _float_key(x):
    bits = int(np.float32(x).view(np.int32))
    return bits ^ 0x7FFFFFFF if bits < 0 else bits


def _indexer_kernel(qi_ref, wi_ref, ki_ref, bias_ref, keys_ref, qs_ref, cut_ref, *, k_sel, tk):
    tq = qi_ref.shape[0]
    qb = pl.program_id(1)
    q0 = qb * tq
    nkv = (q0 + tq + tk - 1) // tk
    for h in range(IDX_HEADS):
        qs_ref[h * tq:(h + 1) * tq, :] = qi_ref[:, h * IDX_DIM:(h + 1) * IDX_DIM]
    w_t = jnp.transpose(wi_ref[...])
    w_rows = [w_t[h:h + 1, :] for h in range(IDX_HEADS)]

    def fill(kb, carry):
        k0 = pl.multiple_of(kb * tk, tk)
        keys_ref[pl.ds(k0, tk), :] = _index_keys(qs_ref[...], w_rows,
                                                 ki_ref[pl.ds(k0, tk), 0:IDX_DIM], q0, k0)
        return carry

    lax.fori_loop(0, nkv, fill, 0)

    row = lax.broadcasted_iota(I32, (tk, tq), 0)

    def block(kb):
        k0 = pl.multiple_of(kb * tk, tk)
        return keys_ref[pl.ds(k0, tk), :], k0 + row

    def count(pred):
        def blk(kb, acc):
            hit = jnp.where(pred(*block(kb)), 1, 0)
            return acc + jnp.sum(hit.reshape(tk // SUBLANES, SUBLANES, tq), axis=0)
        acc = lax.fori_loop(0, nkv, blk, jnp.zeros((SUBLANES, tq), I32))
        return jnp.sum(acc, axis=0, keepdims=True)

    def bit_step(bi, thr):
        cand = thr + jnp.left_shift(jnp.int32(1), 31 - bi)
        n = count(lambda keys, kpos: keys >= cand)
        return jnp.where(n >= k_sel, cand, thr)

    thr = lax.fori_loop(0, 32, bit_step, jnp.full((1, tq), INT_MIN, I32))
    n_ge = count(lambda keys, kpos: keys >= thr)
    no_tie_cut = jnp.full((1, tq), 1 << 30, I32)
    cut_ref[...] = no_tie_cut

    @pl.when(jnp.max(n_ge) > k_sel)
    def _():
        n_gt = count(lambda keys, kpos: keys > thr)
        need = k_sel - n_gt

        def cut_step(bi, cut):
            cand = cut + jnp.left_shift(jnp.int32(1), 15 - bi)
            n = count(lambda keys, kpos: (keys == thr) & (kpos < cand))
            return jnp.where(n <= need, cand, cut)

        cut = lax.fori_loop(0, 16, cut_step, jnp.zeros((1, tq), I32))
        cut_ref[...] = jnp.where(n_ge > k_sel, cut, no_tie_cut)

    bias_ref[...] = jnp.full(bias_ref.shape, NEG, BF16)
    cut = cut_ref[...]
    key_valid = jnp.int32(_float_key(0.5 * NEG))

    def emit(kb, carry):
        k0 = pl.multiple_of(kb * tk, tk)
        keys, kpos = block(kb)
        chosen = ((keys > thr) | ((keys == thr) & (kpos < cut))) & (keys > key_valid)
        bias_ref[pl.ds(k0, tk), :] = jnp.where(chosen, 0.0, NEG).astype(BF16)
        return carry

    lax.fori_loop(0, nkv, emit, 0)


def indexer(qi, wi, ki, b, tp, k_sel):
    tq = INDEX_Q_TILE
    nq = tp // tq
    return pl.pallas_call(
        functools.partial(_indexer_kernel, k_sel=k_sel, tk=TIME_TILE),
        out_shape=jax.ShapeDtypeStruct((b * tp, tp), BF16),
        grid=(b, nq),
        in_specs=[pl.BlockSpec((tq, qi.shape[1]), lambda bi, qb: (bi * nq + qb, 0)),
                  pl.BlockSpec((tq, LANES), lambda bi, qb: (bi * nq + qb, 0)),
                  pl.BlockSpec((tp, LANES), lambda bi, qb: (bi, 0))],
        out_specs=pl.BlockSpec((tp, tq), lambda bi, qb: (bi, qb)),
        scratch_shapes=[pltpu.VMEM((tp, tq), I32), pltpu.VMEM((IDX_HEADS * tq, IDX_DIM), BF16),
                        pltpu.VMEM((1, tq), I32)],
        compiler_params=_cparams(("parallel", "arbitrary")),
        name="indexer",
    )(qi, wi, ki)


def _attn_kernel(qb_ref, kb_ref, q_ref, k_ref, v_ref, bias_ref, o_ref, qs_ref, m_ref, l_ref, acc_ref):
    tq = q_ref.shape[0]
    tk = k_ref.shape[0]
    hd = k_ref.shape[1] // ATT_KV_HEADS
    grows = ATT_GROUPS * tq
    qb = qb_ref[pl.program_id(1)]
    kb = kb_ref[pl.program_id(1)]
    nt = (((1,), (1,)), ((), ()))

    @pl.when(kb == 0)
    def _():
        m_ref[...] = jnp.full_like(m_ref, NEG)
        l_ref[...] = jnp.zeros_like(l_ref)
        acc_ref[...] = jnp.zeros_like(acc_ref)
        for hi in range(ATT_HEADS):
            qs_ref[hi * tq:(hi + 1) * tq, :] = q_ref[:, hi * hd:(hi + 1) * hd]

    bias = jnp.transpose(bias_ref[...].astype(F32))[None]
    ones = jnp.ones((tk, LANES), BF16)
    for kvh in range(ATT_KV_HEADS):
        rows = slice(kvh * grows, (kvh + 1) * grows)
        kh = k_ref[:, kvh * hd:(kvh + 1) * hd]
        vh1 = jnp.concatenate([v_ref[:, kvh * hd:(kvh + 1) * hd], ones], axis=1)
        s = lax.dot_general(qs_ref[rows, :], kh, nt, preferred_element_type=F32)
        s = (s.reshape(ATT_GROUPS, tq, tk) + bias).reshape(grows, tk)
        m_old = m_ref[rows, :]
        m_new = jnp.maximum(m_old, jnp.broadcast_to(jnp.max(s, axis=1, keepdims=True),
                                                    (grows, LANES)))
        p = jnp.concatenate(
            [jnp.exp2(s[:, t * LANES:(t + 1) * LANES] - m_new).astype(BF16)
             for t in range(tk // LANES)], axis=1)
        alpha = jnp.exp2(m_old - m_new)
        pv = jnp.dot(p, vh1, preferred_element_type=F32)
        l_ref[rows, :] = alpha * l_ref[rows, :] + pv[:, hd:]
        acc_ref[rows, :] = alpha * acc_ref[rows, :] + pv[:, :hd]
        m_ref[rows, :] = m_new

    @pl.when(kb == qb)
    def _():
        for hi in range(ATT_HEADS):
            rows = slice(hi * tq, (hi + 1) * tq)
            o_ref[:, hi * hd:(hi + 1) * hd] = (acc_ref[rows, :] / l_ref[rows, :]).astype(o_ref.dtype)


def sparse_attention(qk, z, bias, b, tp, d, col_v):
    tq = tk = TIME_TILE
    nq = tp // tq
    hd = d // ATT_HEADS
    kvw = ATT_KV_HEADS * hd
    pairs = [(qb, kb) for qb in range(nq) for kb in range(qb + 1)]
    qb_of = jnp.asarray([p[0] for p in pairs], I32)
    kb_of = jnp.asarray([p[1] for p in pairs], I32)
    qrow = lambda bi, i, qbs, kbs: bi * nq + qbs[i]
    kvrow = lambda bi, i, qbs, kbs: bi * nq + kbs[i]
    grid_spec = pltpu.PrefetchScalarGridSpec(
        num_scalar_prefetch=2,
        grid=(b, len(pairs)),
        in_specs=[pl.BlockSpec((tq, d), lambda bi, i, qbs, kbs: (qrow(bi, i, qbs, kbs), 0)),
                  pl.BlockSpec((tk, kvw), lambda bi, i, qbs, kbs: (kvrow(bi, i, qbs, kbs), d // kvw)),
                  pl.BlockSpec((tk, kvw),
                               lambda bi, i, qbs, kbs: (kvrow(bi, i, qbs, kbs), col_v // kvw)),
                  pl.BlockSpec((tk, tq), lambda bi, i, qbs, kbs: (kvrow(bi, i, qbs, kbs), qbs[i]))],
        out_specs=pl.BlockSpec((tq, d), lambda bi, i, qbs, kbs: (qrow(bi, i, qbs, kbs), 0)),
        scratch_shapes=[pltpu.VMEM((ATT_HEADS * tq, hd), BF16),
                        pltpu.VMEM((ATT_HEADS * tq, LANES), F32),
                        pltpu.VMEM((ATT_HEADS * tq, LANES), F32),
                        pltpu.VMEM((ATT_HEADS * tq, hd), F32)],
    )
    return pl.pallas_call(
        _attn_kernel,
        out_shape=jax.ShapeDtypeStruct((b * tp, d), BF16),
        grid_spec=grid_spec,
        compiler_params=_cparams(("parallel", "arbitrary")),
        name="sparse_attention",
    )(qb_of, kb_of, qk, qk, z, bias)


def _router_kernel(x_ref, g_ref, w_ref, o_ref, xt_ref):
    tm, d = x_ref.shape
    nchunk = d // LANES
    for c in range(nchunk):
        xt_ref[pl.ds(c, tm, stride=nchunk), :] = x_ref[:, c * LANES:(c + 1) * LANES]
    xn = _rms(x_ref[...], g_ref[...]).astype(BF16)
    logits = jnp.dot(xn, w_ref[...], preferred_element_type=F32)
    lane = lax.broadcasted_iota(I32, logits.shape, 1)
    logits = jnp.where(lane < N_EXPERTS, logits, -jnp.inf)
    v1 = jnp.max(logits, axis=1, keepdims=True)
    e1 = jnp.min(jnp.where(logits == v1, lane, LANES), axis=1, keepdims=True)
    rest = jnp.where(lane == e1, -jnp.inf, logits)
    v2 = jnp.max(rest, axis=1, keepdims=True)
    e2 = jnp.min(jnp.where(rest == v2, lane, LANES), axis=1, keepdims=True)
    ex = jnp.exp(v2 - v1)
    g1 = 1.0 / (1.0 + ex)
    g2 = ex / (1.0 + ex)
    out = jnp.where(lane == 0, g1, jnp.where(lane == 1, g2, 0.0))
    out = jnp.where(lane == 2, e1.astype(F32), jnp.where(lane == 3, e2.astype(F32), out))
    o_ref[...] = out


def router(x, g, w):
    m, d = x.shape
    tm = TIME_TILE
    nchunk = d // LANES
    return pl.pallas_call(
        _router_kernel,
        out_shape=[jax.ShapeDtypeStruct((m, LANES), F32),
                   jax.ShapeDtypeStruct((m * nchunk, LANES), F32)],
        grid=(m // tm,),
        in_specs=[pl.BlockSpec((tm, d), lambda i: (i, 0)),
                  pl.BlockSpec((1, d), lambda i: (0, 0)),
                  pl.BlockSpec((d, LANES), lambda i: (0, 0))],
        out_specs=[pl.BlockSpec((tm, LANES), lambda i: (i, 0)),
                   pl.BlockSpec((tm * nchunk, LANES), lambda i: (i, 0))],
        compiler_params=_cparams(("parallel",)),
        name="router",
    )(x, g, w)


def _combine_norm_kernel(slots_ref, h_ref, y_hbm, route_ref, g_ref, o_ref, ybuf_ref, sem_ref):
    i = pl.program_id(0)
    nblk = pl.num_programs(0)
    tm, d = h_ref.shape
    nchunk = d // LANES
    m = nblk * tm

    def gather(blk):
        slot = blk % 2
        first = _token_gather(slots_ref, blk * tm, tm, y_hbm, ybuf_ref.at[slot, 0],
                              sem_ref.at[slot], nchunk)
        second = _token_gather(slots_ref, m + blk * tm, tm, y_hbm, ybuf_ref.at[slot, 1],
                               sem_ref.at[slot], nchunk)
        return (lambda: (first[0](), second[0]())), (lambda: (first[1](), second[1]()))

    @pl.when(i == 0)
    def _():
        gather(i)[0]()

    gather(i)[1]()

    @pl.when(i + 1 < nblk)
    def _():
        gather(i + 1)[0]()

    y1_ref = ybuf_ref.at[i % 2, 0]
    y2_ref = ybuf_ref.at[i % 2, 1]
    g1 = route_ref[:, 0:1]
    g2 = route_ref[:, 1:2]
    ssq = jnp.zeros((tm, LANES), F32)
    for c in range(nchunk):
        sl = slice(c * LANES, (c + 1) * LANES)
        hc = (h_ref[:, sl] + g1 * _chunk_rows(y1_ref, c, tm, nchunk)
              + g2 * _chunk_rows(y2_ref, c, tm, nchunk))
        o_ref[:, sl] = hc
        ssq = ssq + hc * hc
    inv = lax.rsqrt(jnp.sum(ssq, axis=1, keepdims=True) / d + EPS)
    o_ref[...] = o_ref[...] * inv * g_ref[...]


def combine_norm(h, y_slots, tok_slots, route, g):
    m, d = h.shape
    tm = TIME_TILE
    nchunk = d // LANES
    grid_spec = pltpu.PrefetchScalarGridSpec(
        num_scalar_prefetch=1,
        grid=(m // tm,),
        in_specs=[pl.BlockSpec((tm, d), lambda i, s: (i, 0)),
                  pl.BlockSpec(memory_space=pl.ANY),
                  pl.BlockSpec((tm, LANES), lambda i, s: (i, 0)),
                  pl.BlockSpec((1, d), lambda i, s: (0, 0))],
        out_specs=pl.BlockSpec((tm, d), lambda i, s: (i, 0)),
        scratch_shapes=[pltpu.VMEM((2, 2, tm * nchunk, LANES), F32), pltpu.SemaphoreType.DMA((2,))],
    )
    return pl.pallas_call(
        _combine_norm_kernel,
        out_shape=jax.ShapeDtypeStruct((m, d), F32),
        grid_spec=grid_spec,
        compiler_params=_cparams(("arbitrary",)),
        name="combine_norm",
    )(tok_slots, h, y_slots, route, g)


def _pad_cols(w, n):
    return jnp.pad(w, ((0, 0), (0, n - w.shape[1])))


def _round_up(n, k):
    return -(-n // k) * k


def layer_ab(h, b, tp, norm_g, w_in, conv_w, conv_b, w_r, b_r, w_i, b_i, lam, w_gk, b_gk,
             gla_norm, w_out, ffn_norm, ffn_wg, ffn_wu, ffn_wd):
    m, d = h.shape
    lw = conv_w.shape[1]
    hdk = w_gk.shape[1]
    hdv = d - lw
    n_in = _round_up(w_in.shape[1] - GLA_RANK + LANES, COL_TILE)
    z = norm_matmul(h, norm_g[None], _pad_cols(w_in, n_in).astype(BF16)).reshape(b, tp, n_in)
    a_out = rg_lru(z, conv_w, conv_b[None], w_r.astype(BF16), b_r.reshape(1, lw),
                   w_i.astype(BF16), b_i.reshape(1, lw), lam[None])
    w_gk_p = jnp.pad(w_gk, ((0, LANES - GLA_RANK), (0, 0))).astype(BF16)
    b_out = gla(z, 2 * lw, w_gk_p, b_gk[None], gla_norm[None], hdk, hdv)
    h = matmul_residual([a_out.reshape(m, lw), b_out.reshape(m, hdv)], w_out.astype(BF16), h)
    return ffn_residual(h, ffn_norm[None], ffn_wg.astype(BF16)[None], ffn_wu.astype(BF16)[None],
                        ffn_wd.astype(BF16)[None])


def layer_c(h, b, tp, t_real, norm_g, w_in, w_out):
    m, d = h.shape
    hd = d // ATT_HEADS
    kvw = ATT_KV_HEADS * hd
    nqi = IDX_HEADS * IDX_DIM
    col_v = d + kvw
    col_qi = d + 2 * kvw
    col_kw = col_qi + nqi
    n_in = _round_up(col_kw + LANES, COL_TILE)
    z = norm_matmul(h, norm_g[None], _pad_cols(w_in, n_in).astype(BF16))
    qk, qi, ki, wi = rope(z, rope_tables(tp, hd), tp, d + kvw, nqi, col_qi, col_kw)
    k_sel = min(TOPK_MAX, (t_real - N_META_TOK) // 4)
    bias = indexer(qi, wi, ki, b, tp, k_sel)
    att = sparse_attention(qk, z, bias, b, tp, d, col_v)
    return matmul_residual([att], w_out.astype(BF16), h)


def moe_dispatch(route, tm):
    m = route.shape[0]
    e_flat = route[:, 2:4].astype(I32).reshape(-1)
    onehot = (e_flat[:, None] == jnp.arange(N_EXPERTS)[None, :]).astype(I32)
    rank = jnp.take_along_axis(jnp.cumsum(onehot, axis=0) - onehot, e_flat[:, None], axis=1)[:, 0]
    counts = jnp.sum(onehot, axis=0)
    padded = (counts + tm - 1) // tm * tm
    pad_end = jnp.cumsum(padded)
    dest = (pad_end - padded)[e_flat] + rank
    nblk = -(-(2 * m) // tm) + N_EXPERTS
    slot_tok = jnp.zeros((nblk * tm,), I32).at[dest].set(jnp.arange(2 * m, dtype=I32) // 2)
    nused = (pad_end[-1] // tm).astype(I32)
    blk = jnp.minimum(jnp.arange(nblk, dtype=I32), nused - 1)
    blk_expert = jnp.minimum(jnp.sum((pad_end[None, :] <= (blk * tm)[:, None]).astype(I32), axis=1),
                             N_EXPERTS - 1)
    return slot_tok, dest.reshape(m, 2), blk_expert, nused.reshape(1)


def layer_moe_final(h, norm_g, w_router, wg, wu, wd, final_g):
    m = h.shape[0]
    route, h_tok = router(h, norm_g[None], _pad_cols(w_router, LANES).astype(BF16))
    tm = _pick_tile(m, (ROW_TILE_BIG, TIME_TILE))
    slot_tok, tok_slots, blk_expert, nused = moe_dispatch(route, tm)
    y_slots = ffn_experts(h_tok, slot_tok, norm_g[None], wg.astype(BF16), wu.astype(BF16),
                          wd.astype(BF16), blk_expert, nused, tm)
    return combine_norm(h, y_slots, tok_slots.T.reshape(-1), route, final_g[None])


def kernel(x, meta, ab_norm, ab_w_in, lru_conv_w, lru_conv_b, lru_w_r, lru_b_r, lru_w_i, lru_b_i,
           lru_lam, gla_w_gk, gla_b_gk, gla_norm, ab_w_out, ffn_norm, ffn_w_gate, ffn_w_up,
           ffn_w_down, c_norm, c_w_in, c_w_out, moe_norm, moe_router, moe_w_gate, moe_w_up,
           moe_w_down, final_norm):
    b, seq, d = x.shape
    assert ab_norm.shape[0] == 1 and c_norm.shape[0] == 1, "two-layer trunk only"
    t_real = seq + N_META_TOK
    tp = _round_up(t_real, TIME_TILE)
    h = jnp.concatenate([jnp.broadcast_to(meta.astype(x.dtype)[None], (b, N_META_TOK, d)), x,
                         jnp.zeros((b, tp - t_real, d), x.dtype)], axis=1).reshape(b * tp, d)
    h = layer_ab(h, b, tp, ab_norm[0], ab_w_in[0], lru_conv_w[0], lru_conv_b[0], lru_w_r[0],
                 lru_b_r[0], lru_w_i[0], lru_b_i[0], lru_lam[0], gla_w_gk[0], gla_b_gk[0],
                 gla_norm[0], ab_w_out[0], ffn_norm[0], ffn_w_gate[0], ffn_w_up[0], ffn_w_down[0])
    h = layer_c(h, b, tp, t_real, c_norm[0], c_w_in[0], c_w_out[0])
    out = layer_moe_final(h, moe_norm[0], moe_router[0], moe_w_gate[0], moe_w_up[0],
                          moe_w_down[0], final_norm)
    return out.reshape(b, tp, d)[:, N_META_TOK:t_real]
```

```python
import functools

import jax
import jax.numpy as jnp
import numpy as np
from jax import lax
from jax.experimental import pallas as pl
from jax.experimental.pallas import tpu as pltpu

F32 = jnp.float32
BF16 = jnp.bfloat16
I32 = jnp.int32

N_META_TOK = 16
EPS = 1e-6
ROPE_THETA = 500000.0
LRU_BLOCKS = 4
CONV_W = 4
LRU_C = 8.0
GLA_HEADS = 4
GLA_RANK = 16
GLA_GATE_NORM = 16.0
ATT_HEADS = 16
ATT_KV_HEADS = 4
ATT_GROUPS = ATT_HEADS // ATT_KV_HEADS
IDX_HEADS = 8
IDX_DIM = 64
TOPK_MAX = 256
BIG = 1e30
NEG = -1e30
N_EXPERTS = 8
LOG2_E = 1.4426950408889634

LANES = 128
SUBLANES = 8
TIME_TILE = 384
INDEX_Q_TILE = 128
ROW_TILE_BIG = 768
COL_TILE = 512
GLA_CHUNK = 64
VMEM_LIMIT = 56 * 1024 * 1024
INT_MIN = -(2 ** 31)


def _cparams(sem):
    return pltpu.CompilerParams(dimension_semantics=sem, vmem_limit_bytes=VMEM_LIMIT)


def _rms(x, g):
    return x * lax.rsqrt(jnp.mean(x * x, axis=-1, keepdims=True) + EPS) * g


def _sigmoid(x):
    return 1.0 / (1.0 + jnp.exp(-x))


def _pick_tile(n, candidates):
    for c in candidates:
        if n % c == 0:
            return c
    raise ValueError(f"no tile in {candidates} divides {n}")


def _norm_matmul_kernel(x_ref, g_ref, w_ref, o_ref, xn_ref):
    @pl.when(pl.program_id(1) == 0)
    def _():
        xn_ref[...] = _rms(x_ref[...], g_ref[...]).astype(BF16)

    o_ref[...] = jnp.dot(xn_ref[...], w_ref[...], preferred_element_type=F32).astype(o_ref.dtype)


def norm_matmul(x, g, w):
    m, d = x.shape
    n = w.shape[1]
    tm = _pick_tile(m, (ROW_TILE_BIG, TIME_TILE))
    tn = COL_TILE
    return pl.pallas_call(
        _norm_matmul_kernel,
        out_shape=jax.ShapeDtypeStruct((m, n), BF16),
        grid=(m // tm, n // tn),
        in_specs=[pl.BlockSpec((tm, d), lambda i, j: (i, 0)),
                  pl.BlockSpec((1, d), lambda i, j: (0, 0)),
                  pl.BlockSpec((d, tn), lambda i, j: (0, j))],
        out_specs=pl.BlockSpec((tm, tn), lambda i, j: (i, j)),
        scratch_shapes=[pltpu.VMEM((tm, d), BF16)],
        compiler_params=_cparams(("parallel", "arbitrary")),
        name="norm_matmul",
    )(x, g, w)


def _matmul_res_kernel(*refs, nparts):
    xs = refs[:nparts]
    w_ref, r_ref, o_ref = refs[nparts:]
    acc = r_ref[...]
    k0 = 0
    for x_ref in xs:
        kp = x_ref.shape[1]
        acc = acc + jnp.dot(x_ref[...], w_ref[k0:k0 + kp, :], preferred_element_type=F32)
        k0 += kp
    o_ref[...] = acc


def matmul_residual(parts, w, res):
    m, n = res.shape
    k = w.shape[0]
    tm = _pick_tile(m, (ROW_TILE_BIG, TIME_TILE))
    tn = 1024
    in_specs = [pl.BlockSpec((tm, p.shape[1]), lambda i, j: (i, 0)) for p in parts]
    in_specs += [pl.BlockSpec((k, tn), lambda i, j: (0, j)),
                 pl.BlockSpec((tm, tn), lambda i, j: (i, j))]
    return pl.pallas_call(
        functools.partial(_matmul_res_kernel, nparts=len(parts)),
        out_shape=jax.ShapeDtypeStruct((m, n), F32),
        grid=(m // tm, n // tn),
        in_specs=in_specs,
        out_specs=pl.BlockSpec((tm, tn), lambda i, j: (i, j)),
        compiler_params=_cparams(("parallel", "arbitrary")),
        name="matmul_residual",
    )(*parts, w, res)


def _chunk_rows(ref, c, rows, nchunk):
    return ref[pl.ds(c, rows, stride=nchunk), :]


def _swiglu_partial(xn_ref, wg_ref, wu_ref, wd_ref):
    xn = xn_ref[...]
    gate = jnp.dot(xn, wg_ref[0], preferred_element_type=F32)
    up = jnp.dot(xn, wu_ref[0], preferred_element_type=F32)
    act = (gate * _sigmoid(gate) * up).astype(BF16)
    return jnp.dot(act, wd_ref[0], preferred_element_type=F32)


def _ffn_kernel(x_ref, g_ref, wg_ref, wu_ref, wd_ref, o_ref, xn_ref):
    j = pl.program_id(1)

    @pl.when(j == 0)
    def _():
        xn_ref[...] = _rms(x_ref[...], g_ref[...]).astype(BF16)

    part = _swiglu_partial(xn_ref, wg_ref, wu_ref, wd_ref)

    @pl.when(j == 0)
    def _():
        o_ref[...] = part + x_ref[...]

    @pl.when(j > 0)
    def _():
        o_ref[...] += part


def ffn_residual(x, g, wg, wu, wd):
    m, d = x.shape
    f = wg.shape[2]
    tm = _pick_tile(m, (ROW_TILE_BIG, TIME_TILE))
    tf = COL_TILE
    return pl.pallas_call(
        _ffn_kernel,
        out_shape=jax.ShapeDtypeStruct((m, d), F32),
        grid=(m // tm, f // tf),
        in_specs=[pl.BlockSpec((tm, d), lambda i, j: (i, 0)),
                  pl.BlockSpec((1, d), lambda i, j: (0, 0)),
                  pl.BlockSpec((1, d, tf), lambda i, j: (0, 0, j)),
                  pl.BlockSpec((1, d, tf), lambda i, j: (0, 0, j)),
                  pl.BlockSpec((1, tf, d), lambda i, j: (0, j, 0))],
        out_specs=pl.BlockSpec((tm, d), lambda i, j: (i, 0)),
        scratch_shapes=[pltpu.VMEM((tm, d), BF16)],
        compiler_params=_cparams(("parallel", "arbitrary")),
        name="ffn_residual",
    )(x, g, wg, wu, wd)


def _token_gather(idx_ref, first, count, src_ref, dst_ref, sem, nchunk):
    def copy(r):
        src_row = pl.multiple_of(idx_ref[first + r] * nchunk, nchunk)
        dst_row = pl.multiple_of(r * nchunk, nchunk)
        return pltpu.make_async_copy(src_ref.at[pl.ds(src_row, nchunk)],
                                     dst_ref.at[pl.ds(dst_row, nchunk)], sem)

    def start():
        lax.fori_loop(0, count, lambda r, c: (copy(r).start(), c)[1], 0)

    def wait():
        rows = count * nchunk
        pltpu.make_async_copy(src_ref.at[pl.ds(0, rows)], dst_ref.at[pl.ds(0, rows)], sem).wait()

    return start, wait


def _experts_kernel(eid_ref, nused_ref, tok_ref, x_hbm, g_ref, wg_ref, wu_ref, wd_ref, o_ref,
                    xbuf_ref, sem_ref, xn_ref, acc_ref):
    i = pl.program_id(0)
    j = pl.program_id(1)
    tm, d = xn_ref.shape
    nchunk = d // LANES
    nused = nused_ref[0]

    def gather(blk):
        slot = blk % 2
        return _token_gather(tok_ref, blk * tm, tm, x_hbm, xbuf_ref.at[slot], sem_ref.at[slot],
                             nchunk)

    @pl.when(i < nused)
    def _():
        @pl.when(j == 0)
        def _():
            @pl.when(i == 0)
            def _():
                gather(i)[0]()

            gather(i)[1]()

            @pl.when(i + 1 < nused)
            def _():
                gather(i + 1)[0]()

            x_ref = xbuf_ref.at[i % 2]
            ssq = jnp.zeros((tm, LANES), F32)
            for c in range(nchunk):
                xc = _chunk_rows(x_ref, c, tm, nchunk)
                ssq = ssq + xc * xc
            inv = lax.rsqrt(jnp.sum(ssq, axis=1, keepdims=True) / d + EPS)
            for c in range(nchunk):
                sl = slice(c * LANES, (c + 1) * LANES)
                xn_ref[:, sl] = (_chunk_rows(x_ref, c, tm, nchunk) * inv * g_ref[:, sl]).astype(BF16)

        part = _swiglu_partial(xn_ref, wg_ref, wu_ref, wd_ref)

        @pl.when(j == 0)
        def _():
            acc_ref[...] = part

        @pl.when(j > 0)
        def _():
            acc_ref[...] += part

    @pl.when(j == pl.num_programs(1) - 1)
    def _():
        @pl.when(i >= nused_ref[0])
        def _():
            acc_ref[...] = jnp.zeros_like(acc_ref)

        for c in range(nchunk):
            o_ref[pl.ds(c, tm, stride=nchunk), :] = acc_ref[:, c * LANES:(c + 1) * LANES]


def ffn_experts(x_tok, slot_tok, g, wg, wu, wd, eid, nused, tm):
    d = wg.shape[1]
    f = wg.shape[2]
    nchunk = d // LANES
    nblk = slot_tok.shape[0] // tm
    tf = COL_TILE
    nj = f // tf

    def jj(i, j, nused):
        return jnp.where(i < nused[0], j, nj - 1)

    grid_spec = pltpu.PrefetchScalarGridSpec(
        num_scalar_prefetch=3,
        grid=(nblk, nj),
        in_specs=[pl.BlockSpec(memory_space=pl.ANY),
                  pl.BlockSpec((1, d), lambda i, j, eid, nu, tok: (0, 0)),
                  pl.BlockSpec((1, d, tf), lambda i, j, eid, nu, tok: (eid[i], 0, jj(i, j, nu))),
                  pl.BlockSpec((1, d, tf), lambda i, j, eid, nu, tok: (eid[i], 0, jj(i, j, nu))),
                  pl.BlockSpec((1, tf, d), lambda i, j, eid, nu, tok: (eid[i], jj(i, j, nu), 0))],
        out_specs=pl.BlockSpec((tm * nchunk, LANES), lambda i, j, eid, nu, tok: (i, 0)),
        scratch_shapes=[pltpu.VMEM((2, tm * nchunk, LANES), F32), pltpu.SemaphoreType.DMA((2,)),
                        pltpu.VMEM((tm, d), BF16), pltpu.VMEM((tm, d), F32)],
    )
    return pl.pallas_call(
        _experts_kernel,
        out_shape=jax.ShapeDtypeStruct((nblk * tm * nchunk, LANES), F32),
        grid_spec=grid_spec,
        compiler_params=_cparams(("arbitrary", "arbitrary")),
        name="ffn_experts",
    )(eid, nused, slot_tok, x_tok, g, wg, wu, wd)


def _lru_kernel(xb_ref, yb_ref, cw_ref, cb_ref, wr_ref, br_ref, wi_ref, bi_ref, lam_ref,
                o_ref, xext_ref, h_ref):
    tb, w = xb_ref.shape

    @pl.when(pl.program_id(1) == 0)
    def _():
        xext_ref[0:SUBLANES, :] = jnp.zeros((SUBLANES, w), F32)
        h_ref[...] = jnp.zeros_like(h_ref)

    xext_ref[SUBLANES:SUBLANES + tb, :] = xb_ref[...].astype(F32)
    cw = cw_ref[...]
    xc = cb_ref[...] + cw[CONV_W - 1:CONV_W, :] * xext_ref[SUBLANES:SUBLANES + tb, :]
    for j in range(CONV_W - 1):
        xc = xc + cw[j:j + 1, :] * xext_ref[pl.ds(SUBLANES - (CONV_W - 1) + j, tb), :]
    xext_ref[0:SUBLANES, :] = xext_ref[tb:tb + SUBLANES, :]

    xcb = xc.astype(BF16)
    nb = wr_ref.shape[0]
    bw = w // nb
    r_parts, i_parts = [], []
    for n in range(nb):
        xs = xcb[:, n * bw:(n + 1) * bw]
        r_parts.append(jnp.dot(xs, wr_ref[n], preferred_element_type=F32))
        i_parts.append(jnp.dot(xs, wi_ref[n], preferred_element_type=F32))
    r = _sigmoid(jnp.concatenate(r_parts, axis=1) + br_ref[...])
    ig = _sigmoid(jnp.concatenate(i_parts, axis=1) + bi_ref[...])

    neg_lam = -lam_ref[...]
    softplus = jnp.maximum(neg_lam, 0.0) + jnp.log1p(jnp.exp(-jnp.abs(neg_lam)))
    log_a = -LRU_C * r * softplus
    a = jnp.exp(log_a)
    u = jnp.sqrt(1.0 - a * a) * (ig * xc)

    row = lax.broadcasted_iota(I32, (tb, w), 0)
    s = 1
    while s < tb:
        keep = row >= s
        a_sh = pltpu.roll(a, s, 0)
        u_sh = pltpu.roll(u, s, 0)
        u = jnp.where(keep, a * u_sh + u, u)
        a = jnp.where(keep, a * a_sh, a)
        s *= 2
    h = a * h_ref[0:1, :] + u
    h_ref[0:1, :] = h[tb - 1:tb, :]

    y = yb_ref[...].astype(F32)
    gelu = 0.5 * y * (1.0 + jnp.tanh(0.7978845608028654 * (y + 0.044715 * (y * y * y))))
    o_ref[...] = (h * gelu).astype(o_ref.dtype)


def rg_lru(z, conv_w, conv_b, w_r, b_r, w_i, b_i, lam):
    b, tp, _ = z.shape
    w = conv_w.shape[1]
    tb = TIME_TILE
    nb = w_r.shape[0]
    bw = w // nb
    full = lambda shape: pl.BlockSpec(shape, lambda bi, ti: (0,) * len(shape))
    return pl.pallas_call(
        _lru_kernel,
        out_shape=jax.ShapeDtypeStruct((b, tp, w), BF16),
        grid=(b, tp // tb),
        in_specs=[pl.BlockSpec((None, tb, w), lambda bi, ti: (bi, ti, 0)),
                  pl.BlockSpec((None, tb, w), lambda bi, ti: (bi, ti, 1)),
                  full((CONV_W, w)), full((1, w)),
                  full((nb, bw, bw)), full((1, w)),
                  full((nb, bw, bw)), full((1, w)),
                  full((1, w))],
        out_specs=pl.BlockSpec((None, tb, w), lambda bi, ti: (bi, ti, 0)),
        scratch_shapes=[pltpu.VMEM((tb + 2 * SUBLANES, w), F32), pltpu.VMEM((SUBLANES, w), F32)],
        compiler_params=_cparams(("parallel", "arbitrary")),
        name="rg_lru",
    )(z, z, conv_w, conv_b, w_r, b_r, w_i, b_i, lam)


def _gla_kernel(q_ref, k_ref, v_ref, g_ref, glr_ref, wgk_ref, bgk_ref, ng_ref, o_ref, st_ref,
                *, chunk, heads):
    tb, hdk = q_ref.shape
    dk = hdk // heads
    dv = v_ref.shape[1] // heads
    c = chunk

    @pl.when(pl.program_id(1) == 0)
    def _():
        st_ref[...] = jnp.zeros_like(st_ref)

    row = lax.broadcasted_iota(I32, (c, hdk), 0)
    tri = lax.broadcasted_iota(I32, (c, c), 0) >= lax.broadcasted_iota(I32, (c, c), 1)
    nt = (((1,), (1,)), ((), ()))
    tn = (((0,), (0,)), ((), ()))

    def body(ci, carry):
        sl = pl.ds(pl.multiple_of(ci * c, c), c)
        pre = jnp.dot(glr_ref[sl, :], wgk_ref[...], preferred_element_type=F32) + bgk_ref[...]
        gk = (jnp.minimum(pre, 0.0) - jnp.log1p(jnp.exp(-jnp.abs(pre)))) * (1.0 / GLA_GATE_NORM)
        bc = gk
        s = 1
        while s < c:
            bc = bc + jnp.where(row >= s, pltpu.roll(bc, s, 0), 0.0)
            s *= 2
        q = q_ref[sl, :].astype(F32) * (dk ** -0.5)
        k = k_ref[sl, :].astype(F32)
        v = v_ref[sl, :]
        g = g_ref[sl, :].astype(F32)
        bmid = bc[c // 2:c // 2 + 1, :]
        blast = bc[c - 1:c, :]
        q_inter = (q * jnp.exp(bc)).astype(BF16)
        q_intra = (q * jnp.exp(bc - bmid)).astype(BF16)
        k_intra = (k * jnp.exp(bmid - bc)).astype(BF16)
        k_state = (k * jnp.exp(blast - bc)).astype(BF16)
        decay = jnp.exp(blast)
        outs = []
        for h in range(heads):
            ks = slice(h * dk, (h + 1) * dk)
            vs = slice(h * dv, (h + 1) * dv)
            st = st_ref[h]
            vh = v[:, vs]
            inter = lax.dot_general(q_inter[:, ks], st.astype(BF16), nt, preferred_element_type=F32)
            att = lax.dot_general(q_intra[:, ks], k_intra[:, ks], nt, preferred_element_type=F32)
            att = jnp.where(tri, att, 0.0).astype(BF16)
            o = inter + jnp.dot(att, vh, preferred_element_type=F32)
            st_ref[h] = st * decay[:, ks] + lax.dot_general(vh, k_state[:, ks], tn,
                                                           preferred_element_type=F32)
            o = _rms(o, ng_ref[...])
            gh = g[:, vs]
            outs.append(o * (gh * _sigmoid(gh)))
        o_ref[sl, :] = jnp.concatenate(outs, axis=1).astype(o_ref.dtype)
        return carry

    lax.fori_loop(0, tb // c, body, 0)


def gla(z, col0, w_gk, b_gk, norm_g, hdk, hdv):
    b, tp, _ = z.shape
    tb = TIME_TILE
    cq, ck = col0 // hdk, col0 // hdk + 1
    cv = (col0 + 2 * hdk) // hdv
    cg = cv + 1
    cl = (col0 + 2 * hdk + 2 * hdv) // LANES
    full = lambda shape: pl.BlockSpec(shape, lambda bi, ti: (0,) * len(shape))
    return pl.pallas_call(
        functools.partial(_gla_kernel, chunk=GLA_CHUNK, heads=GLA_HEADS),
        out_shape=jax.ShapeDtypeStruct((b, tp, hdv), BF16),
        grid=(b, tp // tb),
        in_specs=[pl.BlockSpec((None, tb, hdk), lambda bi, ti: (bi, ti, cq)),
                  pl.BlockSpec((None, tb, hdk), lambda bi, ti: (bi, ti, ck)),
                  pl.BlockSpec((None, tb, hdv), lambda bi, ti: (bi, ti, cv)),
                  pl.BlockSpec((None, tb, hdv), lambda bi, ti: (bi, ti, cg)),
                  pl.BlockSpec((None, tb, LANES), lambda bi, ti: (bi, ti, cl)),
                  full((LANES, hdk)), full((1, hdk)), full((1, hdv // GLA_HEADS))],
        out_specs=pl.BlockSpec((None, tb, hdv), lambda bi, ti: (bi, ti, 0)),
        scratch_shapes=[pltpu.VMEM((GLA_HEADS, hdv // GLA_HEADS, hdk // GLA_HEADS), F32)],
        compiler_params=_cparams(("parallel", "arbitrary")),
        name="gla",
    )(z, z, z, z, z, w_gk, b_gk, norm_g)


def _rope_tile(x, cos, sin_up, sin_dn, half):
    return x * cos + pltpu.roll(x, half, 1) * sin_up + pltpu.roll(x, LANES - half, 1) * sin_dn


def _rope_kernel(qk_ref, qi_ref, kw_ref, tab_ref, qk_out, qi_out, ki_out, wi_out):
    def tabs(kind):
        return tab_ref[3 * kind], tab_ref[3 * kind + 1], tab_ref[3 * kind + 2]

    n_q_tiles = ATT_HEADS
    for t in range(qk_ref.shape[1] // LANES):
        ca, sa_up, sa_dn = tabs(0 if t < n_q_tiles else 1)
        sl = slice(t * LANES, (t + 1) * LANES)
        qk_out[:, sl] = _rope_tile(qk_ref[:, sl].astype(F32), ca, sa_up, sa_dn, 16).astype(BF16)
    cb, sb_up, sb_dn = tabs(2)
    for t in range(qi_ref.shape[1] // LANES):
        sl = slice(t * LANES, (t + 1) * LANES)
        qi_out[:, sl] = _rope_tile(qi_ref[:, sl].astype(F32), cb, sb_up, sb_dn, 8).astype(BF16)
    ck, sk_up, sk_dn = tabs(3)
    kw = _rope_tile(kw_ref[...].astype(F32), ck, sk_up, sk_dn, 8)
    ki_out[...] = kw.astype(BF16)
    wi_out[...] = pltpu.roll(kw, LANES - IDX_DIM, 1)


def rope_tables(tp, head_dim):
    pos = jnp.arange(tp, dtype=F32)[:, None]
    lane = jnp.arange(LANES)

    def tables(dim, period, active):
        half = (dim // 4) // 2
        freqs = ROPE_THETA ** (-jnp.arange(half, dtype=F32) / half)
        ang = pos * freqs[None, :]
        cos, sin = jnp.cos(ang), jnp.sin(ang)
        within = lane % period
        fidx = within % half
        cos_l, sin_l = cos[:, fidx], sin[:, fidx]
        lo = (within < half) & active
        hi = (within >= half) & (within < 2 * half) & active
        c = jnp.where((lo | hi)[None, :], cos_l, 1.0)
        s_up = jnp.where(hi[None, :], sin_l, 0.0)
        s_dn = jnp.where(lo[None, :], -sin_l, 0.0)
        return c, s_up, s_dn

    all_on = jnp.ones((LANES,), bool)
    a = tables(head_dim, head_dim, all_on)
    bq = tables(IDX_DIM, IDX_DIM, all_on)
    ck, sk_up, sk_dn = tables(IDX_DIM, IDX_DIM, lane < IDX_DIM)
    wscale = IDX_HEADS ** -0.5 * IDX_DIM ** -0.5
    ck = jnp.where((lane >= IDX_DIM)[None, :],
                   jnp.where(lane < IDX_DIM + IDX_HEADS, wscale, 0.0)[None, :], ck)
    a_scaled = [t * (head_dim ** -0.5 * LOG2_E) for t in a]
    return jnp.stack([*a_scaled, *a, *bq, ck, sk_up, sk_dn])


def rope(z, tabs, tp, nqk, nqi, col_qi, col_kw):
    m = z.shape[0]
    tm = TIME_TILE
    nper = tp // tm
    return pl.pallas_call(
        _rope_kernel,
        out_shape=[jax.ShapeDtypeStruct((m, nqk), BF16), jax.ShapeDtypeStruct((m, nqi), BF16),
                   jax.ShapeDtypeStruct((m, LANES), BF16), jax.ShapeDtypeStruct((m, LANES), F32)],
        grid=(m // tm,),
        in_specs=[pl.BlockSpec((tm, nqk), lambda i: (i, 0)),
                  pl.BlockSpec((tm, nqi), lambda i: (i, col_qi // nqi)),
                  pl.BlockSpec((tm, LANES), lambda i: (i, col_kw // LANES)),
                  pl.BlockSpec((12, tm, LANES), lambda i: (0, i % nper, 0))],
        out_specs=[pl.BlockSpec((tm, nqk), lambda i: (i, 0)),
                   pl.BlockSpec((tm, nqi), lambda i: (i, 0)),
                   pl.BlockSpec((tm, LANES), lambda i: (i, 0)),
                   pl.BlockSpec((tm, LANES), lambda i: (i, 0))],
        compiler_params=_cparams(("parallel",)),
        name="rope",
    )(z, z, z, tabs)


def _index_keys(qi_stack, w_rows, ki_blk, q0, k0):
    nt = (((1,), (1,)), ((), ()))
    tq = w_rows[0].shape[1]
    lg = lax.dot_general(ki_blk, qi_stack, nt, preferred_element_type=F32)
    score = None
    for h, wh in enumerate(w_rows):
        term = wh * jnp.maximum(lg[:, h * tq:(h + 1) * tq], 0.0)
        score = term if score is None else score + term
    tk = score.shape[0]
    kpos = k0 + lax.broadcasted_iota(I32, (tk, tq), 0)
    qpos = q0 + lax.broadcasted_iota(I32, (tk, tq), 1)
    score = jnp.where(kpos <= qpos, jnp.where(kpos < N_META_TOK, BIG, score), NEG)
    bits = pltpu.bitcast(score, I32)
    return jnp.where(bits < 0, bits ^ jnp.int32(0x7FFFFFFF), bits)


def _float_key(x):
    bits = int(np.float32(x).view(np.int32))
    return bits ^ 0x7FFFFFFF if bits < 0 else bits


def _indexer_kernel(qi_ref, wi_ref, ki_ref, bias_ref, keys_ref, qs_ref, cut_ref, *, k_sel, tk):
    tq = qi_ref.shape[0]
    qb = pl.program_id(1)
    q0 = qb * tq
    nkv = (q0 + tq + tk - 1) // tk
    for h in range(IDX_HEADS):
        qs_ref[h * tq:(h + 1) * tq, :] = qi_ref[:, h * IDX_DIM:(h + 1) * IDX_DIM]
    w_t = jnp.transpose(wi_ref[...])
    w_rows = [w_t[h:h + 1, :] for h in range(IDX_HEADS)]

    def fill(kb, carry):
        k0 = pl.multiple_of(kb * tk, tk)
        keys_ref[pl.ds(k0, tk), :] = _index_keys(qs_ref[...], w_rows,
                                                 ki_ref[pl.ds(k0, tk), 0:IDX_DIM], q0, k0)
        return carry

    lax.fori_loop(0, nkv, fill, 0)

    row = lax.broadcasted_iota(I32, (tk, tq), 0)

    def block(kb):
        k0 = pl.multiple_of(kb * tk, tk)
        return keys_ref[pl.ds(k0, tk), :], k0 + row

    def count(pred):
        def blk(kb, acc):
            hit = jnp.where(pred(*block(kb)), 1, 0)
            return acc + jnp.sum(hit.reshape(tk // SUBLANES, SUBLANES, tq), axis=0)
        acc = lax.fori_loop(0, nkv, blk, jnp.zeros((SUBLANES, tq), I32))
        return jnp.sum(acc, axis=0, keepdims=True)

    def bit_step(bi, thr):
        cand = thr + jnp.left_shift(jnp.int32(1), 31 - bi)
        n = count(lambda keys, kpos: keys >= cand)
        return jnp.where(n >= k_sel, cand, thr)

    thr = lax.fori_loop(0, 32, bit_step, jnp.full((1, tq), INT_MIN, I32))
    n_ge = count(lambda keys, kpos: keys >= thr)
    no_tie_cut = jnp.full((1, tq), 1 << 30, I32)
    cut_ref[...] = no_tie_cut

    @pl.when(jnp.max(n_ge) > k_sel)
    def _():
        n_gt = count(lambda keys, kpos: keys > thr)
        need = k_sel - n_gt

        def cut_step(bi, cut):
            cand = cut + jnp.left_shift(jnp.int32(1), 15 - bi)
            n = count(lambda keys, kpos: (keys == thr) & (kpos < cand))
            return jnp.where(n <= need, cand, cut)

        cut = lax.fori_loop(0, 16, cut_step, jnp.zeros((1, tq), I32))
        cut_ref[...] = jnp.where(n_ge > k_sel, cut, no_tie_cut)

    bias_ref[...] = jnp.full(bias_ref.shape, NEG, BF16)
    cut = cut_ref[...]
    key_valid = jnp.int32(_float_key(0.5 * NEG))

    def emit(kb, carry):
        k0 = pl.multiple_of(kb * tk, tk)
        keys, kpos = block(kb)
        chosen = ((keys > thr) | ((keys == thr) & (kpos < cut))) & (keys > key_valid)
        bias_ref[pl.ds(k0, tk), :] = jnp.where(chosen, 0.0, NEG).astype(BF16)
        return carry

    lax.fori_loop(0, nkv, emit, 0)


def indexer(qi, wi, ki, b, tp, k_sel):
    tq = INDEX_Q_TILE
    nq = tp // tq
    return pl.pallas_call(
        functools.partial(_indexer_kernel, k_sel=k_sel, tk=TIME_TILE),
        out_shape=jax.ShapeDtypeStruct((b * tp, tp), BF16),
        grid=(b, nq),
        in_specs=[pl.BlockSpec((tq, qi.shape[1]), lambda bi, qb: (bi * nq + qb, 0)),
                  pl.BlockSpec((tq, LANES), lambda bi, qb: (bi * nq + qb, 0)),
                  pl.BlockSpec((tp, LANES), lambda bi, qb: (bi, 0))],
        out_specs=pl.BlockSpec((tp, tq), lambda bi, qb: (bi, qb)),
        scratch_shapes=[pltpu.VMEM((tp, tq), I32), pltpu.VMEM((IDX_HEADS * tq, IDX_DIM), BF16),
                        pltpu.VMEM((1, tq), I32)],
        compiler_params=_cparams(("parallel", "arbitrary")),
        name="indexer",
    )(qi, wi, ki)


def _attn_kernel(qb_ref, kb_ref, kb2_ref, q_ref, k_ref, v_ref, bias_ref, k2_ref, v2_ref, bias2_ref,
                 o_ref, qs_ref, m_ref, l_ref, acc_ref):
    tq = q_ref.shape[0]
    tk = 2 * k_ref.shape[0]
    hd = k_ref.shape[1] // ATT_KV_HEADS
    grows = ATT_GROUPS * tq
    qb = qb_ref[pl.program_id(1)]
    kb = kb_ref[pl.program_id(1)]
    kb2 = kb2_ref[pl.program_id(1)]
    nt = (((1,), (1,)), ((), ()))

    @pl.when(kb == 0)
    def _():
        m_ref[...] = jnp.full_like(m_ref, NEG)
        l_ref[...] = jnp.zeros_like(l_ref)
        acc_ref[...] = jnp.zeros_like(acc_ref)
        for hi in range(ATT_HEADS):
            qs_ref[hi * tq:(hi + 1) * tq, :] = q_ref[:, hi * hd:(hi + 1) * hd]

    bias2 = jnp.where(kb2 > kb, jnp.transpose(bias2_ref[...].astype(F32)), NEG)
    bias = jnp.concatenate([jnp.transpose(bias_ref[...].astype(F32)), bias2], axis=1)[None]
    ones = jnp.ones((tk, LANES), BF16)
    for kvh in range(ATT_KV_HEADS):
        rows = slice(kvh * grows, (kvh + 1) * grows)
        hs = slice(kvh * hd, (kvh + 1) * hd)
        kh = jnp.concatenate([k_ref[:, hs], k2_ref[:, hs]], axis=0)
        vh1 = jnp.concatenate([jnp.concatenate([v_ref[:, hs], v2_ref[:, hs]], axis=0), ones], axis=1)
        s = lax.dot_general(qs_ref[rows, :], kh, nt, preferred_element_type=F32)
        s = (s.reshape(ATT_GROUPS, tq, tk) + bias).reshape(grows, tk)
        m_old = m_ref[rows, :]
        m_new = jnp.maximum(m_old, jnp.broadcast_to(jnp.max(s, axis=1, keepdims=True),
                                                    (grows, LANES)))
        p = jnp.concatenate(
            [jnp.exp2(s[:, t * LANES:(t + 1) * LANES] - m_new).astype(BF16)
             for t in range(tk // LANES)], axis=1)
        alpha = jnp.exp2(m_old - m_new)
        pv = jnp.dot(p, vh1, preferred_element_type=F32)
        l_ref[rows, :] = alpha * l_ref[rows, :] + pv[:, hd:]
        acc_ref[rows, :] = alpha * acc_ref[rows, :] + pv[:, :hd]
        m_ref[rows, :] = m_new

    @pl.when(kb2 == qb)
    def _():
        for hi in range(ATT_HEADS):
            rows = slice(hi * tq, (hi + 1) * tq)
            o_ref[:, hi * hd:(hi + 1) * hd] = (acc_ref[rows, :] / l_ref[rows, :]).astype(o_ref.dtype)


def sparse_attention(qk, z, bias, b, tp, d, col_v):
    tq = tk = TIME_TILE
    nq = tp // tq
    hd = d // ATT_HEADS
    kvw = ATT_KV_HEADS * hd
    steps = [(qb, kb, min(kb + 1, qb)) for qb in range(nq) for kb in range(0, qb + 1, 2)]
    qb_of = jnp.asarray([s[0] for s in steps], I32)
    kb_of = jnp.asarray([s[1] for s in steps], I32)
    kb2_of = jnp.asarray([s[2] for s in steps], I32)
    qrow = lambda bi, i, qbs: bi * nq + qbs[i]
    kvrow = lambda bi, i, kbs: bi * nq + kbs[i]

    def kv_specs(which):
        pick = (lambda kbs, kb2s: kbs) if which == 0 else (lambda kbs, kb2s: kb2s)
        return [pl.BlockSpec((tk, kvw), lambda bi, i, qbs, kbs, kb2s: (kvrow(bi, i, pick(kbs, kb2s)),
                                                                      d // kvw)),
                pl.BlockSpec((tk, kvw), lambda bi, i, qbs, kbs, kb2s: (kvrow(bi, i, pick(kbs, kb2s)),
                                                                      col_v // kvw)),
                pl.BlockSpec((tk, tq), lambda bi, i, qbs, kbs, kb2s: (kvrow(bi, i, pick(kbs, kb2s)),
                                                                     qbs[i]))]

    grid_spec = pltpu.PrefetchScalarGridSpec(
        num_scalar_prefetch=3,
        grid=(b, len(steps)),
        in_specs=[pl.BlockSpec((tq, d), lambda bi, i, qbs, kbs, kb2s: (qrow(bi, i, qbs), 0))]
        + kv_specs(0) + kv_specs(1),
        out_specs=pl.BlockSpec((tq, d), lambda bi, i, qbs, kbs, kb2s: (qrow(bi, i, qbs), 0)),
        scratch_shapes=[pltpu.VMEM((ATT_HEADS * tq, hd), BF16),
                        pltpu.VMEM((ATT_HEADS * tq, LANES), F32),
                        pltpu.VMEM((ATT_HEADS * tq, LANES), F32),
                        pltpu.VMEM((ATT_HEADS * tq, hd), F32)],
    )
    return pl.pallas_call(
        _attn_kernel,
        out_shape=jax.ShapeDtypeStruct((b * tp, d), BF16),
        grid_spec=grid_spec,
        compiler_params=_cparams(("parallel", "arbitrary")),
        name="sparse_attention",
    )(qb_of, kb_of, kb2_of, qk, qk, z, bias, qk, z, bias)


def _router_kernel(x_ref, g_ref, w_ref, o_ref, xt_ref):
    tm, d = x_ref.shape
    nchunk = d // LANES
    for c in range(nchunk):
        xt_ref[pl.ds(c, tm, stride=nchunk), :] = x_ref[:, c * LANES:(c + 1) * LANES]
    xn = _rms(x_ref[...], g_ref[...]).astype(BF16)
    logits = jnp.dot(xn, w_ref[...], preferred_element_type=F32)
    lane = lax.broadcasted_iota(I32, logits.shape, 1)
    logits = jnp.where(lane < N_EXPERTS, logits, -jnp.inf)
    v1 = jnp.max(logits, axis=1, keepdims=True)
    e1 = jnp.min(jnp.where(logits == v1, lane, LANES), axis=1, keepdims=True)
    rest = jnp.where(lane == e1, -jnp.inf, logits)
    v2 = jnp.max(rest, axis=1, keepdims=True)
    e2 = jnp.min(jnp.where(rest == v2, lane, LANES), axis=1, keepdims=True)
    ex = jnp.exp(v2 - v1)
    g1 = 1.0 / (1.0 + ex)
    g2 = ex / (1.0 + ex)
    out = jnp.where(lane == 0, g1, jnp.where(lane == 1, g2, 0.0))
    out = jnp.where(lane == 2, e1.astype(F32), jnp.where(lane == 3, e2.astype(F32), out))
    o_ref[...] = out


def router(x, g, w):
    m, d = x.shape
    tm = TIME_TILE
    nchunk = d // LANES
    return pl.pallas_call(
        _router_kernel,
        out_shape=[jax.ShapeDtypeStruct((m, LANES), F32),
                   jax.ShapeDtypeStruct((m * nchunk, LANES), F32)],
        grid=(m // tm,),
        in_specs=[pl.BlockSpec((tm, d), lambda i: (i, 0)),
                  pl.BlockSpec((1, d), lambda i: (0, 0)),
                  pl.BlockSpec((d, LANES), lambda i: (0, 0))],
        out_specs=[pl.BlockSpec((tm, LANES), lambda i: (i, 0)),
                   pl.BlockSpec((tm * nchunk, LANES), lambda i: (i, 0))],
        compiler_params=_cparams(("parallel",)),
        name="router",
    )(x, g, w)


def _combine_norm_kernel(slots_ref, h_ref, y_hbm, route_ref, g_ref, o_ref, ybuf_ref, sem_ref):
    i = pl.program_id(0)
    nblk = pl.num_programs(0)
    tm, d = h_ref.shape
    nchunk = d // LANES
    m = nblk * tm

    def gather(blk):
        slot = blk % 2
        first = _token_gather(slots_ref, blk * tm, tm, y_hbm, ybuf_ref.at[slot, 0],
                              sem_ref.at[slot], nchunk)
        second = _token_gather(slots_ref, m + blk * tm, tm, y_hbm, ybuf_ref.at[slot, 1],
                               sem_ref.at[slot], nchunk)
        return (lambda: (first[0](), second[0]())), (lambda: (first[1](), second[1]()))

    @pl.when(i == 0)
    def _():
        gather(i)[0]()

    gather(i)[1]()

    @pl.when(i + 1 < nblk)
    def _():
        gather(i + 1)[0]()

    y1_ref = ybuf_ref.at[i % 2, 0]
    y2_ref = ybuf_ref.at[i % 2, 1]
    g1 = route_ref[:, 0:1]
    g2 = route_ref[:, 1:2]
    ssq = jnp.zeros((tm, LANES), F32)
    for c in range(nchunk):
        sl = slice(c * LANES, (c + 1) * LANES)
        hc = (h_ref[:, sl] + g1 * _chunk_rows(y1_ref, c, tm, nchunk)
              + g2 * _chunk_rows(y2_ref, c, tm, nchunk))
        o_ref[:, sl] = hc
        ssq = ssq + hc * hc
    inv = lax.rsqrt(jnp.sum(ssq, axis=1, keepdims=True) / d + EPS)
    o_ref[...] = o_ref[...] * inv * g_ref[...]


def combine_norm(h, y_slots, tok_slots, route, g):
    m, d = h.shape
    tm = TIME_TILE
    nchunk = d // LANES
    grid_spec = pltpu.PrefetchScalarGridSpec(
        num_scalar_prefetch=1,
        grid=(m // tm,),
        in_specs=[pl.BlockSpec((tm, d), lambda i, s: (i, 0)),
                  pl.BlockSpec(memory_space=pl.ANY),
                  pl.BlockSpec((tm, LANES), lambda i, s: (i, 0)),
                  pl.BlockSpec((1, d), lambda i, s: (0, 0))],
        out_specs=pl.BlockSpec((tm, d), lambda i, s: (i, 0)),
        scratch_shapes=[pltpu.VMEM((2, 2, tm * nchunk, LANES), F32), pltpu.SemaphoreType.DMA((2,))],
    )
    return pl.pallas_call(
        _combine_norm_kernel,
        out_shape=jax.ShapeDtypeStruct((m, d), F32),
        grid_spec=grid_spec,
        compiler_params=_cparams(("arbitrary",)),
        name="combine_norm",
    )(tok_slots, h, y_slots, route, g)


def _pad_cols(w, n):
    return jnp.pad(w, ((0, 0), (0, n - w.shape[1])))


def _round_up(n, k):
    return -(-n // k) * k


def layer_ab(h, b, tp, norm_g, w_in, conv_w, conv_b, w_r, b_r, w_i, b_i, lam, w_gk, b_gk,
             gla_norm, w_out, ffn_norm, ffn_wg, ffn_wu, ffn_wd):
    m, d = h.shape
    lw = conv_w.shape[1]
    hdk = w_gk.shape[1]
    hdv = d - lw
    n_in = _round_up(w_in.shape[1] - GLA_RANK + LANES, COL_TILE)
    z = norm_matmul(h, norm_g[None], _pad_cols(w_in, n_in).astype(BF16)).reshape(b, tp, n_in)
    a_out = rg_lru(z, conv_w, conv_b[None], w_r.astype(BF16), b_r.reshape(1, lw),
                   w_i.astype(BF16), b_i.reshape(1, lw), lam[None])
    w_gk_p = jnp.pad(w_gk, ((0, LANES - GLA_RANK), (0, 0))).astype(BF16)
    b_out = gla(z, 2 * lw, w_gk_p, b_gk[None], gla_norm[None], hdk, hdv)
    h = matmul_residual([a_out.reshape(m, lw), b_out.reshape(m, hdv)], w_out.astype(BF16), h)
    return ffn_residual(h, ffn_norm[None], ffn_wg.astype(BF16)[None], ffn_wu.astype(BF16)[None],
                        ffn_wd.astype(BF16)[None])


def layer_c(h, b, tp, t_real, norm_g, w_in, w_out):
    m, d = h.shape
    hd = d // ATT_HEADS
    kvw = ATT_KV_HEADS * hd
    nqi = IDX_HEADS * IDX_DIM
    col_v = d + kvw
    col_qi = d + 2 * kvw
    col_kw = col_qi + nqi
    n_in = _round_up(col_kw + LANES, COL_TILE)
    z = norm_matmul(h, norm_g[None], _pad_cols(w_in, n_in).astype(BF16))
    qk, qi, ki, wi = rope(z, rope_tables(tp, hd), tp, d + kvw, nqi, col_qi, col_kw)
    k_sel = min(TOPK_MAX, (t_real - N_META_TOK) // 4)
    bias = indexer(qi, wi, ki, b, tp, k_sel)
    att = sparse_attention(qk, z, bias, b, tp, d, col_v)
    return matmul_residual([att], w_out.astype(BF16), h)


def moe_dispatch(route, tm):
    m = route.shape[0]
    e_flat = route[:, 2:4].astype(I32).reshape(-1)
    onehot = (e_flat[:, None] == jnp.arange(N_EXPERTS)[None, :]).astype(I32)
    rank = jnp.take_along_axis(jnp.cumsum(onehot, axis=0) - onehot, e_flat[:, None], axis=1)[:, 0]
    counts = jnp.sum(onehot, axis=0)
    padded = (counts + tm - 1) // tm * tm
    pad_end = jnp.cumsum(padded)
    dest = (pad_end - padded)[e_flat] + rank
    nblk = -(-(2 * m) // tm) + N_EXPERTS
    slot_tok = jnp.zeros((nblk * tm,), I32).at[dest].set(jnp.arange(2 * m, dtype=I32) // 2)
    nused = (pad_end[-1] // tm).astype(I32)
    blk = jnp.minimum(jnp.arange(nblk, dtype=I32), nused - 1)
    blk_expert = jnp.minimum(jnp.sum((pad_end[None, :] <= (blk * tm)[:, None]).astype(I32), axis=1),
                             N_EXPERTS - 1)
    return slot_tok, dest.reshape(m, 2), blk_expert, nused.reshape(1)


def layer_moe_final(h, norm_g, w_router, wg, wu, wd, final_g):
    m = h.shape[0]
    route, h_tok = router(h, norm_g[None], _pad_cols(w_router, LANES).astype(BF16))
    tm = _pick_tile(m, (ROW_TILE_BIG, TIME_TILE))
    slot_tok, tok_slots, blk_expert, nused = moe_dispatch(route, tm)
    y_slots = ffn_experts(h_tok, slot_tok, norm_g[None], wg.astype(BF16), wu.astype(BF16),
                          wd.astype(BF16), blk_expert, nused, tm)
    return combine_norm(h, y_slots, tok_slots.T.reshape(-1), route, final_g[None])


def kernel(x, meta, ab_norm, ab_w_in, lru_conv_w, lru_conv_b, lru_w_r, lru_b_r, lru_w_i, lru_b_i,
           lru_lam, gla_w_gk, gla_b_gk, gla_norm, ab_w_out, ffn_norm, ffn_w_gate, ffn_w_up,
           ffn_w_down, c_norm, c_w_in, c_w_out, moe_norm, moe_router, moe_w_gate, moe_w_up,
           moe_w_down, final_norm):
    b, seq, d = x.shape
    assert ab_norm.shape[0] == 1 and c_norm.shape[0] == 1, "two-layer trunk only"
    t_real = seq + N_META_TOK
    tp = _round_up(t_real, TIME_TILE)
    h = jnp.concatenate([jnp.broadcast_to(meta.astype(x.dtype)[None], (b, N_META_TOK, d)), x,
                         jnp.zeros((b, tp - t_real, d), x.dtype)], axis=1).reshape(b * tp, d)
    h = layer_ab(h, b, tp, ab_norm[0], ab_w_in[0], lru_conv_w[0], lru_conv_b[0], lru_w_r[0],
                 lru_b_r[0], lru_w_i[0], lru_b_i[0], lru_lam[0], gla_w_gk[0], gla_b_gk[0],
                 gla_norm[0], ab_w_out[0], ffn_norm[0], ffn_w_gate[0], ffn_w_up[0], ffn_w_down[0])
    h = layer_c(h, b, tp, t_real, c_norm[0], c_w_in[0], c_w_out[0])
    out = layer_moe_final(h, moe_norm[0], moe_router[0], moe_w_gate[0], moe_w_up[0],
                          moe_w_down[0], final_norm)
    return out.reshape(b, tp, d)[:, N_META_TOK:t_real]
```
